```python
import jax, jax.numpy as jnp
from jax import lax
import numpy as np

D_MODEL = 2048
BATCH = 1
SEQ = 8192
DEPTH = 1
DEC_BATCH = 32
DEC_SEQ = 1
PAST_LEN = 16384
PAGE_SIZE = 128

CONV_C = D_MODEL
CONV_K = 31
N_HEADS = 16
N_KV = 4
GROUP = N_HEADS // N_KV
HEAD_DIM = 128
ATT_WIDTH = N_HEADS * HEAD_DIM
KV_WIDTH = N_KV * HEAD_DIM
CMP_LEN = 32
CMP_STRIDE = 16
SEL_BLOCK = 64
N_SEL = 16
N_LOCAL = 2
WINDOW = 512
Q_BLOCK = 128
RMS_EPS = 1e-6
LN_EPS = 1e-5
FORCE_BONUS = 1e9
IN_SIZES = (CONV_C, CONV_C, CONV_C, ATT_WIDTH) + (KV_WIDTH,) * 6 + (ATT_WIDTH, 3 * N_HEADS, D_MODEL, D_MODEL)
IN_WIDTH = sum(IN_SIZES)

kernel_name = 'hybrid_conformerconv_nsa_alibi_step'


def _rmsnorm(x, g):
    xf = x.astype(jnp.float32)
    y = xf * lax.rsqrt(jnp.mean(xf * xf, axis=-1, keepdims=True) + RMS_EPS)
    return (y * g.astype(jnp.float32)).astype(x.dtype)


def _layernorm(x, g, b):
    xf = x.astype(jnp.float32)
    xc = xf - jnp.mean(xf, axis=-1, keepdims=True)
    y = xc * lax.rsqrt(jnp.mean(xc * xc, axis=-1, keepdims=True) + LN_EPS)
    return (y * g.astype(jnp.float32) + b.astype(jnp.float32)).astype(x.dtype)


def _alibi_slopes():
    h = jnp.arange(1, N_HEADS + 1, dtype=jnp.float32)
    return jnp.exp2(-8.0 * h / N_HEADS).reshape(1, N_KV, GROUP, 1, 1)


def _masked_softmax(s, mask):
    s = jnp.where(mask, s, -jnp.inf)
    m = jnp.max(s, axis=-1, keepdims=True)
    m = jnp.where(jnp.isfinite(m), m, 0.0)
    p = jnp.exp(s - m)
    return p / jnp.maximum(jnp.sum(p, axis=-1, keepdims=True), 1e-30)


def _project(x, g_pre, w_in):
    h = _rmsnorm(x, g_pre)
    idx = [int(v) for v in np.cumsum(IN_SIZES)[:-1]]
    return jnp.split(h @ w_in, idx, axis=-1)


def _conv_mix(u_hist, zc, w_dw, b_dw, ln_g, ln_b, w_conv_out):
    c = lax.conv_general_dilated(u_hist, w_dw[:, None, :].astype(u_hist.dtype), window_strides=(1,), padding='VALID',
                                 dimension_numbers=('NWC', 'WIO', 'NWC'), feature_group_count=CONV_C)
    c = _layernorm(c + b_dw, ln_g, ln_b)
    return (jax.nn.silu(c) * jax.nn.silu(zc)) @ w_conv_out


def _compress(rows, pe, w1, w2):
    b, length, g, dh = rows.shape
    nc = (length - CMP_LEN) // CMP_STRIDE + 1
    r_n = CMP_LEN // CMP_STRIDE
    n_sub = nc + r_n - 1
    sub = rows[:, :n_sub * CMP_STRIDE].reshape(b, n_sub, CMP_STRIDE, g, dh)
    w1l = w1.reshape(CMP_LEN, dh, dh)
    w1r = w1.reshape(r_n, CMP_STRIDE, dh, dh)
    hid = jnp.einsum('ld,lde->e', pe, w1l)
    for r in range(r_n):
        a = jnp.einsum('bnsgd,sde->bnge', sub, w1r[r])
        hid = hid + a[:, r:r + nc]
    return jnp.einsum('bnge,ef->bngf', jax.nn.silu(hid), w2)


def _cmp_positions(nc):
    start = jnp.arange(nc) * CMP_STRIDE
    return start + CMP_LEN - 1, start.astype(jnp.float32) + (CMP_LEN - 1) / 2


def _cmp_to_sel(nc, ns):
    i = jnp.arange(nc)[:, None] * CMP_STRIDE
    j = jnp.arange(ns)[None, :] * SEL_BLOCK
    ov = jnp.clip(jnp.minimum(i + CMP_LEN, j + SEL_BLOCK) - jnp.maximum(i, j), 0, None)
    return ov.astype(jnp.float32) / CMP_LEN


def _nsa_attend(q, qpos, kc, vc, c_end, c_ctr, ovl, gather_sel, kw, vw, wpos, gate_logits):
    b, tq = q.shape[:2]
    scale = HEAD_DIM ** -0.5
    slopes = _alibi_slopes()
    qf = qpos.astype(jnp.float32)
    s = jnp.einsum('bqgrd,bngd->bgrqn', q, kc, preferred_element_type=jnp.float32) * scale
    s = s - slopes * jnp.abs(qf[:, None] - c_ctr[None, :])
    p_c = _masked_softmax(s, c_end[None, :] <= qpos[:, None])
    o_c = jnp.einsum('bgrqn,bngd->bqgrd', p_c, vc.astype(jnp.float32))
    ns = ovl.shape[1]
    imp = jnp.einsum('bgqn,nj->bgqj', jnp.sum(p_c, axis=2), ovl)
    j = jnp.arange(ns)[None, :]
    cur = (qpos // SEL_BLOCK)[:, None]
    valid = j * SEL_BLOCK <= qpos[:, None]
    forced = (j == 0) | ((j <= cur) & (j > cur - N_LOCAL))
    score = jnp.where(valid, imp + jnp.where(forced, FORCE_BONUS, 0.0), -FORCE_BONUS)
    _, top = lax.top_k(score, min(N_SEL, ns))
    pos = (top[..., None] * SEL_BLOCK + jnp.arange(SEL_BLOCK)).reshape(b, N_KV, tq, -1)
    ks, vs = gather_sel(jnp.transpose(pos, (0, 2, 1, 3)))
    s = jnp.einsum('bqgrd,bqgsd->bgrqs', q, ks, preferred_element_type=jnp.float32) * scale
    s = s - slopes * jnp.abs(qf[None, None, :, None] - pos.astype(jnp.float32))[:, :, None]
    p_s = _masked_softmax(s, (pos <= qpos[None, None, :, None])[:, :, None])
    o_s = jnp.einsum('bgrqs,bqgsd->bqgrd', p_s, vs.astype(jnp.float32))
    s = jnp.einsum('bqgrd,bkgd->bgrqk', q, kw, preferred_element_type=jnp.float32) * scale
    s = s - slopes * jnp.abs(qf[:, None] - wpos.astype(jnp.float32)[None, :])
    dq = qpos[:, None] - wpos[None, :]
    p_w = _masked_softmax(s, (dq >= 0) & (dq < WINDOW) & (wpos[None, :] >= 0))
    o_w = jnp.einsum('bgrqk,bkgd->bqgrd', p_w, vw.astype(jnp.float32))
    g = jax.nn.sigmoid(gate_logits.astype(jnp.float32)).reshape(b, tq, 3, N_KV, GROUP, 1)
    o = g[:, :, 0] * o_c + g[:, :, 1] * o_s + g[:, :, 2] * o_w
    return o.reshape(b, tq, ATT_WIDTH).astype(q.dtype)


def _merge(x, conv_d, o, za, mg_c, mg_a, w_attn_out, w_out, g_post):
    attn_d = (o * jax.nn.silu(za)) @ w_attn_out
    m = jax.nn.sigmoid(mg_c) * conv_d + jax.nn.sigmoid(mg_a) * attn_d
    return x + _rmsnorm(m @ w_out, g_post)


def _prompt_layer(x, g_pre, w_in, w_dw, b_dw, ln_g, ln_b, w_conv_out, pe_k, w1_k, w2_k, pe_v, w1_v, w2_v,
                  w_attn_out, w_out, g_post):
    b, t, _ = x.shape
    (u, ug, zc, q, kcr, vcr, ksr, vsr, kwr, vwr, za, gl, mg_c, mg_a) = _project(x, g_pre, w_in)
    u = u * jax.nn.sigmoid(ug)
    conv_d = _conv_mix(jnp.pad(u, ((0, 0), (CONV_K - 1, 0), (0, 0))), zc, w_dw, b_dw, ln_g, ln_b, w_conv_out)
    heads = lambda a: a.reshape(b, t, N_KV, HEAD_DIM)
    k_c, v_c, k_s, v_s, k_w, v_w = heads(kcr), heads(vcr), heads(ksr), heads(vsr), heads(kwr), heads(vwr)
    kc = _compress(k_c, pe_k, w1_k, w2_k)
    vc = _compress(v_c, pe_v, w1_v, w2_v)
    nc = kc.shape[1]
    c_end, c_ctr = _cmp_positions(nc)
    ovl = _cmp_to_sel(nc, -(-t // SEL_BLOCK))
    q5 = q.reshape(b, t, N_KV, GROUP, HEAD_DIM)
    bi = jnp.arange(b)[:, None, None, None]
    gi = jnp.arange(N_KV)[None, None, :, None]

    def gather_sel(pos):
        pc = jnp.clip(pos, 0, t - 1)
        return k_s[bi, pc, gi], v_s[bi, pc, gi]

    pad = ((0, 0), (WINDOW, 0), (0, 0), (0, 0))
    kw_pad, vw_pad = jnp.pad(k_w, pad), jnp.pad(v_w, pad)

    def block(i):
        s0 = i * Q_BLOCK
        qpos = s0 + jnp.arange(Q_BLOCK)
        wpos = s0 - WINDOW + jnp.arange(WINDOW + Q_BLOCK)
        sl = lambda a: lax.dynamic_slice_in_dim(a, s0, Q_BLOCK, axis=1)
        kw = lax.dynamic_slice_in_dim(kw_pad, s0, WINDOW + Q_BLOCK, axis=1)
        vw = lax.dynamic_slice_in_dim(vw_pad, s0, WINDOW + Q_BLOCK, axis=1)
        return _nsa_attend(sl(q5), qpos, kc, vc, c_end, c_ctr, ovl, gather_sel, kw, vw, wpos, sl(gl))

    o = lax.map(block, jnp.arange(t // Q_BLOCK))
    o = jnp.moveaxis(o, 0, 1).reshape(b, t, ATT_WIDTH)
    y = _merge(x, conv_d, o, za, mg_c, mg_a, w_attn_out, w_out, g_post)
    wl = min(WINDOW, t)
    return y, (k_c, v_c, k_s, v_s, k_w[:, -wl:], v_w[:, -wl:], u[:, -(CONV_K - 1):])


def _sample_layer(x, l, ck_cmp, cv_cmp, ck_slc, cv_slc, ck_win, cv_win, c_conv, page_table,
                  g_pre, w_in, w_dw, b_dw, ln_g, ln_b, w_conv_out, pe_k, w1_k, w2_k, pe_v, w1_v, w2_v,
                  w_attn_out, w_out, g_post):
    b, t, _ = x.shape
    n_pages = page_table.shape[1]
    past_len = n_pages * PAGE_SIZE
    (u, ug, zc, q, kcr, vcr, ksr, vsr, kwr, vwr, za, gl, mg_c, mg_a) = _project(x, g_pre, w_in)
    u = u * jax.nn.sigmoid(ug)
    u_hist = jnp.concatenate([c_conv[l], u], axis=1)
    conv_d = _conv_mix(u_hist, zc, w_dw, b_dw, ln_g, ln_b, w_conv_out)
    heads = lambda a: a.reshape(b, t, N_KV, HEAD_DIM)
    k_c, v_c, k_s, v_s, k_w, v_w = heads(kcr), heads(vcr), heads(ksr), heads(vsr), heads(kwr), heads(vwr)
    past = lambda cache: cache[l, page_table].reshape(b, past_len, N_KV, HEAD_DIM)
    kc = _compress(jnp.concatenate([past(ck_cmp), k_c], axis=1), pe_k, w1_k, w2_k)
    vc = _compress(jnp.concatenate([past(cv_cmp), v_c], axis=1), pe_v, w1_v, w2_v)
    nc = kc.shape[1]
    c_end, c_ctr = _cmp_positions(nc)
    ovl = _cmp_to_sel(nc, -(-(past_len + t) // SEL_BLOCK))
    bi = jnp.arange(b)[:, None, None, None]
    gi = jnp.arange(N_KV)[None, None, :, None]

    def gather_sel(pos):
        is_past = (pos < past_len)[..., None]
        pp = jnp.clip(pos, 0, past_len - 1)
        phys = page_table[bi, pp // PAGE_SIZE]
        off = pp % PAGE_SIZE
        pn = jnp.clip(pos - past_len, 0, t - 1)
        ks = jnp.where(is_past, ck_slc[l, phys, off, gi], k_s[bi, pn, gi])
        vs = jnp.where(is_past, cv_slc[l, phys, off, gi], v_s[bi, pn, gi])
        return ks, vs

    w_buf = ck_win.shape[2]
    kw = jnp.concatenate([ck_win[l], k_w], axis=1)
    vw = jnp.concatenate([cv_win[l], v_w], axis=1)
    wpos = past_len - w_buf + jnp.arange(w_buf + t)
    qpos = past_len + jnp.arange(t)
    o = _nsa_attend(q.reshape(b, t, N_KV, GROUP, HEAD_DIM), qpos, kc, vc, c_end, c_ctr, ovl, gather_sel,
                    kw, vw, wpos, gl)
    y = _merge(x, conv_d, o, za, mg_c, mg_a, w_attn_out, w_out, g_post)
    return y, (k_c, v_c, k_s, v_s, kw[:, -w_buf:], vw[:, -w_buf:], u_hist[:, -(CONV_K - 1):])


def setup_inputs(seed: int = 0) -> dict:
    key = jax.random.key(seed)
    ks = jax.random.split(key, 32)
    f32 = jnp.float32
    nrm = lambda k, shape, scale=1.0: scale * jax.random.normal(k, shape, f32)
    n_pages = PAST_LEN // PAGE_SIZE
    n_used = DEC_BATCH * n_pages
    n_pool = n_used + max(1, n_used // 4)
    w_buf = min(WINDOW, PAST_LEN)
    pool = (DEPTH, n_pool, PAGE_SIZE, N_KV, HEAD_DIM)
    page_table = jax.random.permutation(ks[0], n_pool)[:n_used].reshape(DEC_BATCH, n_pages).astype(jnp.int32)
    return {
        'x_prompt': nrm(ks[1], (BATCH, SEQ, D_MODEL)),
        'x_sample': nrm(ks[2], (DEC_BATCH, DEC_SEQ, D_MODEL)),
        'cache_k_cmp': nrm(ks[3], pool),
        'cache_v_cmp': nrm(ks[4], pool),
        'cache_k_slc': nrm(ks[5], pool),
        'cache_v_slc': nrm(ks[6], pool),
        'cache_k_win': nrm(ks[7], (DEPTH, DEC_BATCH, w_buf, N_KV, HEAD_DIM)),
        'cache_v_win': nrm(ks[8], (DEPTH, DEC_BATCH, w_buf, N_KV, HEAD_DIM)),
        'cache_conv': nrm(ks[9], (DEPTH, DEC_BATCH, CONV_K - 1, CONV_C), 0.5),
        'page_table': page_table,
        'g_pre': 1.0 + nrm(ks[10], (DEPTH, D_MODEL), 0.02),
        'w_in': nrm(ks[11], (DEPTH, D_MODEL, IN_WIDTH), D_MODEL ** -0.5),
        'w_dw': nrm(ks[12], (DEPTH, CONV_K, CONV_C), CONV_K ** -0.5),
        'b_dw': nrm(ks[13], (DEPTH, CONV_C), 0.01),
        'ln_g': 1.0 + nrm(ks[14], (DEPTH, CONV_C), 0.02),
        'ln_b': nrm(ks[15], (DEPTH, CONV_C), 0.01),
        'w_conv_out': nrm(ks[16], (DEPTH, CONV_C, D_MODEL), CONV_C ** -0.5),
        'pe_k': nrm(ks[17], (DEPTH, CMP_LEN, HEAD_DIM), 0.1),
        'w1_k': nrm(ks[18], (DEPTH, CMP_LEN * HEAD_DIM, HEAD_DIM), (CMP_LEN * HEAD_DIM) ** -0.5),
        'w2_k': nrm(ks[19], (DEPTH, HEAD_DIM, HEAD_DIM), HEAD_DIM ** -0.5),
        'pe_v': nrm(ks[20], (DEPTH, CMP_LEN, HEAD_DIM), 0.1),
        'w1_v': nrm(ks[21], (DEPTH, CMP_LEN * HEAD_DIM, HEAD_DIM), (CMP_LEN * HEAD_DIM) ** -0.5),
        'w2_v': nrm(ks[22], (DEPTH, HEAD_DIM, HEAD_DIM), HEAD_DIM ** -0.5),
        'w_attn_out': nrm(ks[23], (DEPTH, ATT_WIDTH, D_MODEL), ATT_WIDTH ** -0.5),
        'w_out': nrm(ks[24], (DEPTH, D_MODEL, D_MODEL), D_MODEL ** -0.5),
        'g_post': 1.0 + nrm(ks[25], (DEPTH, D_MODEL), 0.02),
    }


def reference(x_prompt, x_sample, cache_k_cmp, cache_v_cmp, cache_k_slc, cache_v_slc, cache_k_win, cache_v_win,
              cache_conv, page_table, g_pre, w_in, w_dw, b_dw, ln_g, ln_b, w_conv_out, pe_k, w1_k, w2_k,
              pe_v, w1_v, w2_v, w_attn_out, w_out, g_post):
    y_prompt, y_sample = x_prompt, x_sample
    p_states, s_states = [], []
    for l in range(DEPTH):
        w = (g_pre[l], w_in[l], w_dw[l], b_dw[l], ln_g[l], ln_b[l], w_conv_out[l], pe_k[l], w1_k[l], w2_k[l],
             pe_v[l], w1_v[l], w2_v[l], w_attn_out[l], w_out[l], g_post[l])
        y_prompt, sp = _prompt_layer(y_prompt, *w)
        y_sample, ss = _sample_layer(y_sample, l, cache_k_cmp, cache_v_cmp, cache_k_slc, cache_v_slc,
                                     cache_k_win, cache_v_win, cache_conv, page_table, *w)
        p_states.append(sp)
        s_states.append(ss)
    (p_k_cmp, p_v_cmp, p_k_slc, p_v_slc, p_k_win, p_v_win, p_conv) = [jnp.stack(a) for a in zip(*p_states)]
    (s_k_cmp, s_v_cmp, s_k_slc, s_v_slc, s_k_win, s_v_win, s_conv) = [jnp.stack(a) for a in zip(*s_states)]
    return (y_prompt, y_sample, p_k_cmp, p_v_cmp, p_k_slc, p_v_slc, p_k_win, p_v_win, p_conv,
            s_k_cmp, s_v_cmp, s_k_slc, s_v_slc, s_k_win, s_v_win, s_conv)
```

```python
import functools

import numpy as np
import jax
import jax.numpy as jnp
from jax import lax
from jax.experimental import pallas as pl
from jax.experimental.pallas import tpu as pltpu

F32 = jnp.float32
BF16 = jnp.bfloat16

D_MODEL = 2048
CONV_C = D_MODEL
CONV_K = 31
N_HEADS = 16
N_KV = 4
GROUP = N_HEADS // N_KV
HEAD_DIM = 128
ATT_WIDTH = N_HEADS * HEAD_DIM
KV_WIDTH = N_KV * HEAD_DIM
CMP_LEN = 32
CMP_STRIDE = 16
SEL_BLOCK = 64
N_SEL = 16
N_LOCAL = 2
WINDOW = 512
RMS_EPS = 1e-6
LN_EPS = 1e-5
FORCE_BONUS = 1e9
PAGE_SIZE = 128
IN_SIZES = (CONV_C, CONV_C, CONV_C, ATT_WIDTH) + (KV_WIDTH,) * 6 + (ATT_WIDTH, 3 * N_HEADS, D_MODEL, D_MODEL)

LANE = 128
SUBLANE = 8
NEG_INF = float("-inf")
SCALE = HEAD_DIM ** -0.5

C_U, C_UG, C_ZC, C_Q, C_ZA, C_MGC, C_MGA = (i * D_MODEL for i in range(7))
C_KC = 7 * D_MODEL
C_VC, C_KS, C_VS, C_KW, C_VW = (C_KC + i * KV_WIDTH for i in range(1, 6))
C_GL = C_KC + 6 * KV_WIDTH
PROJ_TN = 1280
PROJ_W = 14 * PROJ_TN
VMEM_LIMIT = 56 * 1024 * 1024


def _cparams(sem):
    return pltpu.CompilerParams(dimension_semantics=sem, vmem_limit_bytes=VMEM_LIMIT)


def _const_spec(shape):
    nd = len(shape)
    return pl.BlockSpec(shape, lambda *a: (0,) * nd, pipeline_mode=pl.Buffered(1))


def _sigmoid(x):
    return jax.nn.sigmoid(x)


def _silu(x):
    return x * jax.nn.sigmoid(x)


def _dot_t(a, b):
    return lax.dot_general(a, b, (((1,), (1,)), ((), ())), preferred_element_type=F32)


def _stack_rows(rows, n):
    w = rows[0].shape[1]
    rid = lax.broadcasted_iota(jnp.int32, (n, 1), 0)
    out = jnp.zeros((n, w), rows[0].dtype)
    for r, v in enumerate(rows):
        out = jnp.where(rid == r, jnp.broadcast_to(v, (n, w)), out)
    return out


def _take_row(x, r):
    rid = lax.broadcasted_iota(jnp.int32, (x.shape[0], 1), 0)
    return jnp.sum(jnp.where(rid == r, x, 0.0), axis=0, keepdims=True)


def _proj_kernel(x_ref, g_ref, w_ref, o_ref, h_ref):
    @pl.when(pl.program_id(1) == 0)
    def _():
        x = x_ref[...]
        ms = jnp.mean(x * x, axis=-1, keepdims=True)
        h_ref[...] = (x * lax.rsqrt(ms + RMS_EPS) * g_ref[...]).astype(BF16)

    o_ref[...] = jnp.dot(h_ref[...], w_ref[...], preferred_element_type=F32)


def _project(x2d, g_pre, w_r):
    m = x2d.shape[0]
    tm = min(512, m)
    return pl.pallas_call(
        _proj_kernel,
        grid=(m // tm, PROJ_W // PROJ_TN),
        in_specs=[
            pl.BlockSpec((tm, D_MODEL), lambda i, j: (i, 0)),
            pl.BlockSpec((1, D_MODEL), lambda i, j: (0, 0)),
            pl.BlockSpec((D_MODEL, PROJ_TN), lambda i, j: (0, j)),
        ],
        out_specs=pl.BlockSpec((tm, PROJ_TN), lambda i, j: (i, j)),
        out_shape=jax.ShapeDtypeStruct((m, PROJ_W), F32),
        scratch_shapes=[pltpu.VMEM((tm, D_MODEL), BF16)],
        compiler_params=_cparams(("parallel", "arbitrary")),
        name="proj",
    )(x2d, g_pre.reshape(1, D_MODEL), w_r)


HALO = 32
CONV_RB = 8


def _ln_gate_out(c, zc, bdw, lng, lnb, wco_ref):
    c = c + bdw
    mean = jnp.mean(c, axis=-1, keepdims=True)
    xc = c - mean
    var = jnp.mean(xc * xc, axis=-1, keepdims=True)
    y = xc * lax.rsqrt(var + LN_EPS) * lng + lnb
    act = _silu(y) * _silu(zc)
    return jnp.dot(act.astype(BF16), wco_ref[...], preferred_element_type=F32)


def _conv_kernel(u_ref, ug_ref, zc_ref, uh_ref, ugh_ref, wdw_ref, bdw_ref, lng_ref, lnb_ref, wco_ref,
                 out_ref, tail_ref, uext_ref, sh_ref, c_ref):
    i = pl.program_id(0)
    tm = u_ref.shape[0]
    glu = u_ref[...] * _sigmoid(ug_ref[...])
    halo = uh_ref[...] * _sigmoid(ugh_ref[...])
    not_first = (jnp.zeros((HALO, 1), jnp.int32) + i) > 0
    uext_ref[0:HALO, :] = jnp.where(not_first, halo, 0.0)
    uext_ref[HALO:HALO + tm, :] = glu
    tail_ref[...] = glu[tm - HALO:, :]
    n_sh = tm + HALO - SUBLANE
    for p in range(1, SUBLANE):
        sh_ref[p - 1] = uext_ref[p:p + n_sh, :]

    off = HALO - (CONV_K - 1)

    def body(rc, carry):
        r0 = pl.multiple_of(rc * CONV_RB, CONV_RB)
        acc = jnp.zeros((CONV_RB, CONV_C), F32)
        for k in range(CONV_K):
            p, base = (off + k) % SUBLANE, (off + k) // SUBLANE * SUBLANE
            if p == 0:
                win = uext_ref[pl.ds(r0 + base, CONV_RB), :]
            else:
                win = sh_ref[p - 1, pl.ds(r0 + base, CONV_RB), :]
            acc = acc + win * wdw_ref[k:k + 1, :]
        c_ref[pl.ds(r0, CONV_RB), :] = acc
        return carry

    lax.fori_loop(0, tm // CONV_RB, body, 0)
    out_ref[...] = _ln_gate_out(c_ref[...], zc_ref[...], bdw_ref[...], lng_ref[...], lnb_ref[...], wco_ref)


def _conv_branch(proj, w_dw_p, b_dw, ln_g, ln_b, w_co):
    t = proj.shape[0]
    tm = min(256, t)
    hb = tm // HALO
    row = lambda a: a.reshape(1, CONV_C)
    return pl.pallas_call(
        _conv_kernel,
        grid=(t // tm,),
        in_specs=[
            pl.BlockSpec((tm, CONV_C), lambda i: (i, C_U // CONV_C)),
            pl.BlockSpec((tm, CONV_C), lambda i: (i, C_UG // CONV_C)),
            pl.BlockSpec((tm, CONV_C), lambda i: (i, C_ZC // CONV_C)),
            pl.BlockSpec((HALO, CONV_C), lambda i: (jnp.maximum(i * hb - 1, 0), C_U // CONV_C)),
            pl.BlockSpec((HALO, CONV_C), lambda i: (jnp.maximum(i * hb - 1, 0), C_UG // CONV_C)),
            _const_spec((HALO, CONV_C)),
            _const_spec((1, CONV_C)),
            _const_spec((1, CONV_C)),
            _const_spec((1, CONV_C)),
            _const_spec((CONV_C, D_MODEL)),
        ],
        out_specs=[
            pl.BlockSpec((tm, D_MODEL), lambda i: (i, 0)),
            pl.BlockSpec((HALO, CONV_C), lambda i: (0, 0)),
        ],
        out_shape=[jax.ShapeDtypeStruct((t, D_MODEL), F32), jax.ShapeDtypeStruct((HALO, CONV_C), F32)],
        scratch_shapes=[pltpu.VMEM((tm + HALO, CONV_C), F32),
                        pltpu.VMEM((SUBLANE - 1, tm + HALO - SUBLANE, CONV_C), F32),
                        pltpu.VMEM((tm, CONV_C), F32)],
        compiler_params=_cparams(("arbitrary",)),
        name="conv_branch",
    )(proj, proj, proj, proj, proj, w_dw_p, row(b_dw), row(ln_g), row(ln_b), w_co)


def _sconv_kernel(cc_ref, u_ref, ug_ref, zc_ref, wdw_ref, bdw_ref, lng_ref, lnb_ref, wco_ref,
                  out_ref, state_ref):
    nh = CONV_K - 1
    glu = u_ref[...] * _sigmoid(ug_ref[...])
    c = glu * wdw_ref[nh:nh + 1, :]
    for k in range(nh):
        c = c + cc_ref[k] * wdw_ref[k:k + 1, :]
    out_ref[...] = _ln_gate_out(c, zc_ref[...], bdw_ref[...], lng_ref[...], lnb_ref[...], wco_ref)
    for k in range(nh - 1):
        state_ref[k] = cc_ref[k + 1]
    state_ref[nh - 1] = glu


def _sample_conv(cc, sproj, w_dw_p, b_dw, ln_g, ln_b, w_co):
    nh, b = cc.shape[0], cc.shape[1]
    row = lambda a: a.reshape(1, CONV_C)
    return pl.pallas_call(
        _sconv_kernel,
        grid=(1,),
        in_specs=[
            pl.BlockSpec((nh, b, CONV_C), lambda i: (0, 0, 0)),
            pl.BlockSpec((b, CONV_C), lambda i: (0, C_U // CONV_C)),
            pl.BlockSpec((b, CONV_C), lambda i: (0, C_UG // CONV_C)),
            pl.BlockSpec((b, CONV_C), lambda i: (0, C_ZC // CONV_C)),
            pl.BlockSpec((HALO, CONV_C), lambda i: (0, 0)),
            pl.BlockSpec((1, CONV_C), lambda i: (0, 0)),
            pl.BlockSpec((1, CONV_C), lambda i: (0, 0)),
            pl.BlockSpec((1, CONV_C), lambda i: (0, 0)),
            pl.BlockSpec((CONV_C, D_MODEL), lambda i: (0, 0)),
        ],
        out_specs=[
            pl.BlockSpec((b, D_MODEL), lambda i: (0, 0)),
            pl.BlockSpec((nh, b, CONV_C), lambda i: (0, 0, 0)),
        ],
        out_shape=[jax.ShapeDtypeStruct((b, D_MODEL), F32), jax.ShapeDtypeStruct((nh, b, CONV_C), F32)],
        compiler_params=_cparams(("arbitrary",)),
        name="sample_conv",
    )(cc, sproj, sproj, sproj, w_dw_p, row(b_dw), row(ln_g), row(ln_b), w_co)


SUB_PER_PAGE = PAGE_SIZE // CMP_STRIDE
PAGE_VIEW_W = CMP_STRIDE * KV_WIDTH
R_N = CMP_LEN // CMP_STRIDE


def _compress_kernel(pt_ref, *refs, pps):
    del pt_ref
    pages = (refs[:pps], refs[pps:2 * pps])
    w01s = refs[2 * pps:2 * pps + 2]
    pes = refs[2 * pps + 2:2 * pps + 4]
    w2s = refs[2 * pps + 4:2 * pps + 6]
    outs = refs[2 * pps + 6:2 * pps + 8]
    carries = refs[2 * pps + 8:2 * pps + 10]
    nb = pps * SUB_PER_PAGE

    @pl.when(pl.program_id(1) == 0)
    def _():
        carries[0][...] = jnp.zeros_like(carries[0])
        carries[1][...] = jnp.zeros_like(carries[1])

    row = lax.broadcasted_iota(jnp.int32, (nb, HEAD_DIM), 0)
    for which in range(2):
        x = jnp.concatenate([p[...] for p in pages[which]], axis=0)
        xg = [jnp.concatenate([x[:, (s * N_KV + g) * HEAD_DIM:(s * N_KV + g + 1) * HEAD_DIM]
                               for s in range(CMP_STRIDE)], axis=1) for g in range(N_KV)]
        xs = jnp.concatenate(xg, axis=0).astype(BF16)
        w01 = w01s[which][...]
        a = jnp.dot(xs, w01, preferred_element_type=F32)
        pe_o = jnp.dot(pes[which][...].astype(BF16), w01, preferred_element_type=F32)
        hid0 = pe_o[0:1, :HEAD_DIM] + pe_o[1:2, HEAD_DIM:]
        carry = carries[which]
        for g in range(N_KV):
            a0 = a[g * nb:(g + 1) * nb, :HEAD_DIM]
            a1 = a[g * nb:(g + 1) * nb, HEAD_DIM:]
            a0s = jnp.where(row == 0, carry[g:g + 1, :], pltpu.roll(a0, 1, 0))
            carry[g:g + 1, :] = a0[nb - 1:nb, :]
            hid = hid0 + a0s + a1
            out = jnp.dot(_silu(hid).astype(BF16), w2s[which][...], preferred_element_type=F32)
            outs[which][:, g * HEAD_DIM:(g + 1) * HEAD_DIM] = out


def _w01(w1):
    w = w1.reshape(R_N, CMP_STRIDE * HEAD_DIM, HEAD_DIM)
    return jnp.concatenate([w[r] for r in range(R_N)], axis=1).astype(BF16)


def _pe2(pe):
    p = pe.reshape(R_N, CMP_STRIDE * HEAD_DIM)
    return jnp.concatenate([p, jnp.zeros((SUBLANE - R_N, CMP_STRIDE * HEAD_DIM), F32)], axis=0)


def _compress(page_table, cache_k, cache_v, pe_k, w1_k, w2_k, pe_v, w1_v, w2_v, pps):
    b, n_pages = page_table.shape
    nb = pps * SUB_PER_PAGE
    n_sub = n_pages * SUB_PER_PAGE

    def page_spec(i):
        return pl.BlockSpec((None, SUB_PER_PAGE, PAGE_VIEW_W), lambda bb, c, pt: (pt[bb, c * pps + i], 0, 0))

    cspec = lambda shape: pl.BlockSpec(shape, lambda bb, c, pt: (0,) * len(shape))
    in_specs = [page_spec(i) for i in range(pps)] * 2 + [
        cspec((CMP_STRIDE * HEAD_DIM, R_N * HEAD_DIM))] * 2 + [
        cspec((SUBLANE, CMP_STRIDE * HEAD_DIM))] * 2 + [cspec((HEAD_DIM, HEAD_DIM))] * 2
    out_spec = pl.BlockSpec((None, nb, KV_WIDTH), lambda bb, c, pt: (bb, c, 0))
    return pl.pallas_call(
        functools.partial(_compress_kernel, pps=pps),
        grid_spec=pltpu.PrefetchScalarGridSpec(
            num_scalar_prefetch=1,
            grid=(b, n_pages // pps),
            in_specs=in_specs,
            out_specs=[out_spec, out_spec],
            scratch_shapes=[pltpu.VMEM((SUBLANE, HEAD_DIM), F32)] * 2,
        ),
        out_shape=[jax.ShapeDtypeStruct((b, n_sub, KV_WIDTH), F32)] * 2,
        compiler_params=_cparams(("parallel", "arbitrary")),
        name="compress",
    )(page_table, *([cache_k] * pps), *([cache_v] * pps), _w01(w1_k), _w01(w1_v), _pe2(pe_k), _pe2(pe_v),
      w2_k.astype(BF16), w2_v.astype(BF16))


def _ovl_shifted(n_sub, n_lanes):
    i = (np.arange(n_sub)[:, None] - 1) * CMP_STRIDE
    j = np.arange(n_lanes)[None, :] * SEL_BLOCK
    ov = np.clip(np.minimum(i + CMP_LEN, j + SEL_BLOCK) - np.maximum(i, j), 0, None).astype(np.float32) / CMP_LEN
    ov[0, :] = 0.0
    return jnp.asarray(ov, dtype=BF16)


def _softmax_rows(s, valid):
    s = jnp.where(valid, s, NEG_INF)
    mx = jnp.max(s, axis=-1, keepdims=True)
    mx = jnp.where(mx == NEG_INF, 0.0, mx)
    p = jnp.exp(s - mx)
    den = jnp.maximum(jnp.sum(p, axis=-1, keepdims=True), 1e-30)
    return p * (1.0 / den)


def _select_blocks(imp, qpos, n_sel_blocks):
    r, l = imp.shape
    lane = lax.broadcasted_iota(jnp.int32, (r, l), 1)
    lanef = lane.astype(F32)
    cur = lax.shift_right_logical(qpos, int(np.log2(SEL_BLOCK)))
    valid = lane * SEL_BLOCK <= qpos
    forced = (lane == 0) | ((lane <= cur) & (lane > cur - N_LOCAL))
    score = jnp.where(valid, imp + jnp.where(forced, FORCE_BONUS, 0.0), -FORCE_BONUS)
    score = jnp.where(lane < n_sel_blocks, score, -3e38)
    member = jnp.zeros((r, l), F32)
    picks = []
    for _ in range(N_SEL):
        mx = jnp.max(score, axis=-1, keepdims=True)
        idx = jnp.min(jnp.where(score == mx, lanef, float(l)), axis=-1, keepdims=True)
        hit = lanef == idx
        member = jnp.where(hit, 1.0, member)
        score = jnp.where(hit, NEG_INF, score)
        picks.append(idx)
    return picks, member


def _attn1_kernel(sl_ref, q_ref, kc_ref, vc_ref, ovl_ref, oc_ref, mem_ref, *, tq, n_sub, n_sel_blocks):
    qi = pl.program_id(0)
    qpos = qi * tq + lax.broadcasted_iota(jnp.int32, (tq, 1), 0)
    m = lax.broadcasted_iota(jnp.int32, (1, n_sub), 1)
    c_end = (m - 1) * CMP_STRIDE + (CMP_LEN - 1)
    c_ctr = ((m - 1) * CMP_STRIDE).astype(F32) + (CMP_LEN - 1) / 2
    valid = (m >= 1) & (c_end <= qpos)
    dist = jnp.abs(qpos.astype(F32) - c_ctr)
    for g in range(N_KV):
        qg = jnp.concatenate([q_ref[:, (g * GROUP + r) * HEAD_DIM:(g * GROUP + r + 1) * HEAD_DIM]
                              for r in range(GROUP)], axis=0).astype(BF16)
        kg = kc_ref[:, g * HEAD_DIM:(g + 1) * HEAD_DIM].astype(BF16)
        s = _dot_t(qg, kg)
        ps = []
        for r in range(GROUP):
            sr = s[r * tq:(r + 1) * tq] * SCALE - sl_ref[g * GROUP + r] * dist
            ps.append(_softmax_rows(sr, valid))
        psum = ps[0]
        for r in range(1, GROUP):
            psum = psum + ps[r]
        pcat = jnp.concatenate(ps, axis=0).astype(BF16)
        oc = jnp.dot(pcat, vc_ref[:, g * HEAD_DIM:(g + 1) * HEAD_DIM].astype(BF16), preferred_element_type=F32)
        for r in range(GROUP):
            oc_ref[:, (g * GROUP + r) * HEAD_DIM:(g * GROUP + r + 1) * HEAD_DIM] = oc[r * tq:(r + 1) * tq]
        imp = jnp.dot(psum.astype(BF16), ovl_ref[...], preferred_element_type=F32)
        _, member = _select_blocks(imp, qpos, n_sel_blocks)
        mem_ref[g] = member.astype(BF16)


def _prompt_attn1(slopes, proj, kcs, vcs):
    t = proj.shape[0]
    n_sub = kcs.shape[0]
    tq = min(128, t)
    n_sel_blocks = -(-t // SEL_BLOCK)
    assert n_sel_blocks <= LANE
    ovl = _ovl_shifted(n_sub, LANE)
    return pl.pallas_call(
        functools.partial(_attn1_kernel, tq=tq, n_sub=n_sub, n_sel_blocks=n_sel_blocks),
        grid_spec=pltpu.PrefetchScalarGridSpec(
            num_scalar_prefetch=1,
            grid=(t // tq,),
            in_specs=[
                pl.BlockSpec((tq, ATT_WIDTH), lambda i, sl: (i, C_Q // ATT_WIDTH)),
                pl.BlockSpec((n_sub, KV_WIDTH), lambda i, sl: (0, 0)),
                pl.BlockSpec((n_sub, KV_WIDTH), lambda i, sl: (0, 0)),
                pl.BlockSpec((n_sub, LANE), lambda i, sl: (0, 0)),
            ],
            out_specs=[
                pl.BlockSpec((tq, ATT_WIDTH), lambda i, sl: (i, 0)),
                pl.BlockSpec((N_KV, tq, LANE), lambda i, sl: (0, i, 0)),
            ],
        ),
        out_shape=[jax.ShapeDtypeStruct((t, ATT_WIDTH), F32), jax.ShapeDtypeStruct((N_KV, t, LANE), BF16)],
        compiler_params=_cparams(("parallel",)),
        name="prompt_attn_cmp",
    )(slopes, proj, kcs, vcs, ovl)


def _flash_kernel(sl_ref, *refs, tq, tk, window, n_win):
    if window:
        q_ref, k_ref, v_ref, o_ref, qs_ref, m_ref, l_ref, acc_ref = refs
    else:
        q_ref, k_ref, v_ref, mem_ref, e_ref, o_ref, qs_ref, m_ref, l_ref, acc_ref = refs
    g = pl.program_id(0)
    qi = pl.program_id(1)
    kj = pl.program_id(2)
    if window:
        kt = qi * (tq // tk) - n_win + 1 + kj
        first, last = 0, n_win - 1
        active = kt >= 0
    else:
        kt = kj
        first = 0
        last = (qi * tq + tq - 1) // tk
        active = kj <= last

    @pl.when(kj == first)
    def _():
        for r in range(GROUP):
            qs_ref[r * tq:(r + 1) * tq, :] = q_ref[:, r * HEAD_DIM:(r + 1) * HEAD_DIM].astype(BF16)
        m_ref[...] = jnp.full(m_ref.shape, NEG_INF, F32)
        l_ref[...] = jnp.zeros(l_ref.shape, F32)
        acc_ref[...] = jnp.zeros(acc_ref.shape, F32)

    @pl.when(active)
    def _():
        s = _dot_t(qs_ref[...], k_ref[...].astype(BF16))
        qpos = qi * tq + lax.broadcasted_iota(jnp.int32, (tq, 1), 0)
        kpos = kt * tk + lax.broadcasted_iota(jnp.int32, (1, tk), 1)
        rel = kpos - qpos
        if window:
            mask = (rel <= 0) & (rel > -WINDOW)
        else:
            sel = jnp.dot(mem_ref[...], e_ref[...], preferred_element_type=F32) > 0.5
            mask = sel & (rel <= 0)
        relf = rel.astype(F32)
        ps = []
        for r in range(GROUP):
            rows = slice(r * tq, (r + 1) * tq)
            sr = s[rows] * SCALE + sl_ref[g * GROUP + r] * relf
            sr = jnp.where(mask, sr, NEG_INF)
            m_prev = m_ref[rows]
            m_new = jnp.maximum(m_prev, jnp.max(sr, axis=-1, keepdims=True))
            m_safe = jnp.where(m_new == NEG_INF, 0.0, m_new)
            p = jnp.exp(sr - m_safe)
            alpha = jnp.exp(m_prev - m_safe)
            l_ref[rows] = alpha * l_ref[rows] + jnp.sum(p, axis=-1, keepdims=True)
            acc_ref[rows] = alpha * acc_ref[rows]
            m_ref[rows] = m_new
            ps.append(p.astype(BF16))
        pcat = jnp.concatenate(ps, axis=0)
        acc_ref[...] += jnp.dot(pcat, v_ref[...].astype(BF16), preferred_element_type=F32)

    @pl.when(kj == last)
    def _():
        for r in range(GROUP):
            rows = slice(r * tq, (r + 1) * tq)
            o_ref[:, r * HEAD_DIM:(r + 1) * HEAD_DIM] = acc_ref[rows] * (1.0 / jnp.maximum(l_ref[rows], 1e-30))


def _prompt_flash(slopes, proj, member, emat, window):
    t = proj.shape[0]
    if window:
        tq = tk = min(256, t)
        n_win = WINDOW // tk + 1
        nk = n_win
        ck, cv = C_KW, C_VW
        kt_of = lambda qi, kj: jnp.maximum(qi * (tq // tk) - n_win + 1 + kj, 0)
    else:
        tq, tk = min(256, t), min(512, t)
        n_win = 0
        nk = t // tk
        ck, cv = C_KS, C_VS
        kt_of = lambda qi, kj: jnp.minimum(kj, (qi * tq + tq - 1) // tk)
    in_specs = [
        pl.BlockSpec((tq, GROUP * HEAD_DIM), lambda g, qi, kj, sl: (qi, C_Q // (GROUP * HEAD_DIM) + g)),
        pl.BlockSpec((tk, HEAD_DIM), lambda g, qi, kj, sl: (kt_of(qi, kj), ck // HEAD_DIM + g)),
        pl.BlockSpec((tk, HEAD_DIM), lambda g, qi, kj, sl: (kt_of(qi, kj), cv // HEAD_DIM + g)),
    ]
    args = [proj, proj, proj]
    if not window:
        in_specs += [
            pl.BlockSpec((None, tq, LANE), lambda g, qi, kj, sl: (g, qi, 0)),
            pl.BlockSpec((LANE, tk), lambda g, qi, kj, sl: (0, kt_of(qi, kj))),
        ]
        args += [member, emat]
    return pl.pallas_call(
        functools.partial(_flash_kernel, tq=tq, tk=tk, window=window, n_win=n_win),
        grid_spec=pltpu.PrefetchScalarGridSpec(
            num_scalar_prefetch=1,
            grid=(N_KV, t // tq, nk),
            in_specs=in_specs,
            out_specs=pl.BlockSpec((tq, GROUP * HEAD_DIM), lambda g, qi, kj, sl: (qi, g)),
            scratch_shapes=[
                pltpu.VMEM((GROUP * tq, HEAD_DIM), BF16),
                pltpu.VMEM((GROUP * tq, 1), F32),
                pltpu.VMEM((GROUP * tq, 1), F32),
                pltpu.VMEM((GROUP * tq, HEAD_DIM), F32),
            ],
        ),
        out_shape=jax.ShapeDtypeStruct((t, ATT_WIDTH), F32),
        compiler_params=_cparams(("parallel", "parallel", "arbitrary")),
        name="prompt_attn_win" if window else "prompt_attn_sel",
    )(slopes, *args)


def _sel_expand_matrix(t):
    e = (np.arange(t)[None, :] // SEL_BLOCK) == np.arange(LANE)[:, None]
    return jnp.asarray(e, dtype=BF16)


def _sattn1_kernel(sl_ref, q_ref, kc_ref, vc_ref, ovl_ref, oc_ref, idx_ref, *, n_sub, qpos, n_sel_blocks, n_lanes):
    m = lax.broadcasted_iota(jnp.int32, (1, n_sub), 1)
    c_end = (m - 1) * CMP_STRIDE + (CMP_LEN - 1)
    c_ctr = ((m - 1) * CMP_STRIDE).astype(F32) + (CMP_LEN - 1) / 2
    valid = (m >= 1) & (c_end <= qpos)
    dist = jnp.abs(float(qpos) - c_ctr)
    qposv = jnp.full((SUBLANE, 1), qpos, jnp.int32)
    out_lane = lax.broadcasted_iota(jnp.int32, (SUBLANE, LANE), 1)
    out_row = lax.broadcasted_iota(jnp.int32, (SUBLANE, LANE), 0)
    idx_acc = jnp.zeros((SUBLANE, LANE), F32)
    head_row = lax.broadcasted_iota(jnp.int32, (SUBLANE, 1), 0)
    for g in range(N_KV):
        qg = _stack_rows([q_ref[:, (g * GROUP + r) * HEAD_DIM:(g * GROUP + r + 1) * HEAD_DIM]
                          for r in range(GROUP)], SUBLANE).astype(BF16)
        s = _dot_t(qg, kc_ref[:, g * HEAD_DIM:(g + 1) * HEAD_DIM].astype(BF16))
        slope = jnp.zeros((SUBLANE, 1), F32)
        for r in range(GROUP):
            slope = jnp.where(head_row == r, sl_ref[g * GROUP + r], slope)
        p = _softmax_rows(s * SCALE - slope * dist, valid)
        p = jnp.where(head_row < GROUP, p, 0.0)
        oc = jnp.dot(p.astype(BF16), vc_ref[:, g * HEAD_DIM:(g + 1) * HEAD_DIM].astype(BF16),
                     preferred_element_type=F32)
        for r in range(GROUP):
            oc_ref[:, (g * GROUP + r) * HEAD_DIM:(g * GROUP + r + 1) * HEAD_DIM] = _take_row(oc, r)
        psum = jnp.sum(p, axis=0, keepdims=True)
        psum8 = jnp.broadcast_to(psum, (SUBLANE, n_sub))
        imp = jnp.dot(psum8.astype(BF16), ovl_ref[...], preferred_element_type=F32)
        picks, _ = _select_blocks(imp, qposv, n_sel_blocks)
        for j, pk in enumerate(picks):
            idx_acc = jnp.where((out_row == g) & (out_lane == j), pk, idx_acc)
    idx_ref[...] = idx_acc.astype(jnp.int32)


def _sample_attn1(slopes, sproj3, kcs, vcs, past_len):
    b = sproj3.shape[0]
    n_sub = kcs.shape[1]
    n_sel_blocks = -(-(past_len + 1) // SEL_BLOCK)
    n_lanes = -(-n_sel_blocks // LANE) * LANE
    ovl = _ovl_shifted(n_sub, n_lanes)
    return pl.pallas_call(
        functools.partial(_sattn1_kernel, n_sub=n_sub, qpos=past_len, n_sel_blocks=n_sel_blocks, n_lanes=n_lanes),
        grid_spec=pltpu.PrefetchScalarGridSpec(
            num_scalar_prefetch=1,
            grid=(b,),
            in_specs=[
                pl.BlockSpec((None, 1, ATT_WIDTH), lambda i, sl: (i, 0, C_Q // ATT_WIDTH)),
                pl.BlockSpec((None, n_sub, KV_WIDTH), lambda i, sl: (i, 0, 0)),
                pl.BlockSpec((None, n_sub, KV_WIDTH), lambda i, sl: (i, 0, 0)),
                pl.BlockSpec((n_sub, n_lanes), lambda i, sl: (0, 0)),
            ],
            out_specs=[
                pl.BlockSpec((None, 1, ATT_WIDTH), lambda i, sl: (i, 0, 0)),
                pl.BlockSpec((None, SUBLANE, LANE), lambda i, sl: (i, 0, 0)),
            ],
        ),
        out_shape=[jax.ShapeDtypeStruct((b, 1, ATT_WIDTH), F32), jax.ShapeDtypeStruct((b, SUBLANE, LANE), jnp.int32)],
        compiler_params=_cparams(("parallel",)),
        name="sample_attn_cmp",
    )(slopes, sproj3, kcs, vcs, ovl)


def _sattn2_kernel(idx_ref, pt_ref, sl_ref, q_ref, ksn_ref, vsn_ref, kb_ref, vb_ref, kwn_ref, vwn_ref, kwc_ref,
                   vwc_ref, os_ref, ow_ref, kwo_ref, vwo_ref, qs_ref, m_ref, l_ref, acc_ref, *, past_len, w_buf):
    del pt_ref
    b = pl.program_id(0)
    g = pl.program_id(1)
    j = pl.program_id(2)
    head_row = lax.broadcasted_iota(jnp.int32, (SUBLANE, 1), 0)
    slope = jnp.zeros((SUBLANE, 1), F32)
    for r in range(GROUP):
        slope = jnp.where(head_row == r, sl_ref[g * GROUP + r], slope)

    @pl.when(j == 0)
    def _():
        qs_ref[...] = _stack_rows([q_ref[:, r * HEAD_DIM:(r + 1) * HEAD_DIM] for r in range(GROUP)],
                                  SUBLANE).astype(BF16)
        m_ref[...] = jnp.full(m_ref.shape, NEG_INF, F32)
        l_ref[...] = jnp.zeros(l_ref.shape, F32)
        acc_ref[...] = jnp.zeros(acc_ref.shape, F32)

    blk = idx_ref[b, g, j]
    is_past = (blk * SEL_BLOCK + lax.broadcasted_iota(jnp.int32, (SEL_BLOCK, 1), 0)) < past_len
    kblk = jnp.where(is_past, kb_ref[...], jnp.broadcast_to(ksn_ref[...], (SEL_BLOCK, HEAD_DIM)))
    vblk = jnp.where(is_past, vb_ref[...], jnp.broadcast_to(vsn_ref[...], (SEL_BLOCK, HEAD_DIM)))
    kpos = blk * SEL_BLOCK + lax.broadcasted_iota(jnp.int32, (1, SEL_BLOCK), 1)
    rel = kpos - past_len
    s = _dot_t(qs_ref[...], kblk.astype(BF16)) * SCALE + slope * rel.astype(F32)
    s = jnp.where(rel <= 0, s, NEG_INF)
    m_prev = m_ref[...]
    m_new = jnp.maximum(m_prev, jnp.max(s, axis=-1, keepdims=True))
    m_safe = jnp.where(m_new == NEG_INF, 0.0, m_new)
    p = jnp.exp(s - m_safe)
    alpha = jnp.exp(m_prev - m_safe)
    l_ref[...] = alpha * l_ref[...] + jnp.sum(p, axis=-1, keepdims=True)
    acc_ref[...] = alpha * acc_ref[...] + jnp.dot(p.astype(BF16), vblk.astype(BF16), preferred_element_type=F32)
    m_ref[...] = m_new

    @pl.when(j == N_SEL - 1)
    def _():
        o = acc_ref[...] * (1.0 / jnp.maximum(l_ref[...], 1e-30))
        for r in range(GROUP):
            os_ref[:, r * HEAD_DIM:(r + 1) * HEAD_DIM] = _take_row(o, r)
        kw = jnp.concatenate([kwc_ref[...], jnp.broadcast_to(kwn_ref[...], (SUBLANE, HEAD_DIM))], axis=0)
        vw = jnp.concatenate([vwc_ref[...], jnp.broadcast_to(vwn_ref[...], (SUBLANE, HEAD_DIM))], axis=0)
        i = lax.broadcasted_iota(jnp.int32, (1, w_buf + SUBLANE), 1)
        relw = jnp.minimum(i, w_buf) - w_buf
        validw = (relw > -WINDOW) & (i <= w_buf)
        sw = _dot_t(qs_ref[...], kw.astype(BF16)) * SCALE + slope * relw.astype(F32)
        pw = _softmax_rows(sw, validw)
        ow = jnp.dot(pw.astype(BF16), vw.astype(BF16), preferred_element_type=F32)
        for r in range(GROUP):
            ow_ref[:, r * HEAD_DIM:(r + 1) * HEAD_DIM] = _take_row(ow, r)
        rowi = lax.broadcasted_iota(jnp.int32, (w_buf, HEAD_DIM), 0)
        kwo_ref[...] = jnp.where(rowi == w_buf - 1, kwn_ref[...], pltpu.roll(kwc_ref[...], w_buf - 1, 0))
        vwo_ref[...] = jnp.where(rowi == w_buf - 1, vwn_ref[...], pltpu.roll(vwc_ref[...], w_buf - 1, 0))


def _sample_attn2(idx, page_table, slopes, sproj3, pool_k, pool_v, win_k, win_v, past_len):
    b = sproj3.shape[0]
    w_buf = win_k.shape[1]
    n_blk = past_len // SEL_BLOCK
    bpp = PAGE_SIZE // SEL_BLOCK

    def blk_map(bb, g, j, ix, pt, sl):
        blk = jnp.minimum(ix[bb, g, j], n_blk - 1)
        return (pt[bb, blk // bpp], blk % bpp, g)

    new_row = lambda col: pl.BlockSpec((None, 1, HEAD_DIM), lambda bb, g, j, ix, pt, sl: (bb, 0, col // HEAD_DIM + g))
    win_spec = pl.BlockSpec((None, w_buf, HEAD_DIM), lambda bb, g, j, ix, pt, sl: (bb, 0, g))
    o_spec = pl.BlockSpec((None, 1, GROUP * HEAD_DIM), lambda bb, g, j, ix, pt, sl: (bb, 0, g))
    return pl.pallas_call(
        functools.partial(_sattn2_kernel, past_len=past_len, w_buf=w_buf),
        grid_spec=pltpu.PrefetchScalarGridSpec(
            num_scalar_prefetch=3,
            grid=(b, N_KV, N_SEL),
            in_specs=[
                pl.BlockSpec((None, 1, GROUP * HEAD_DIM),
                             lambda bb, g, j, ix, pt, sl: (bb, 0, C_Q // (GROUP * HEAD_DIM) + g)),
                new_row(C_KS), new_row(C_VS),
                pl.BlockSpec((None, SEL_BLOCK, HEAD_DIM), blk_map),
                pl.BlockSpec((None, SEL_BLOCK, HEAD_DIM), blk_map),
                new_row(C_KW), new_row(C_VW),
                win_spec, win_spec,
            ],
            out_specs=[o_spec, o_spec, win_spec, win_spec],
            scratch_shapes=[
                pltpu.VMEM((SUBLANE, HEAD_DIM), BF16),
                pltpu.VMEM((SUBLANE, 1), F32),
                pltpu.VMEM((SUBLANE, 1), F32),
                pltpu.VMEM((SUBLANE, HEAD_DIM), F32),
            ],
        ),
        out_shape=[jax.ShapeDtypeStruct((b, 1, ATT_WIDTH), F32)] * 2
        + [jax.ShapeDtypeStruct((b, w_buf, KV_WIDTH), F32)] * 2,
        compiler_params=_cparams(("parallel", "parallel", "arbitrary")),
        name="sample_attn_sel_win",
    )(idx, page_table, slopes, sproj3, sproj3, sproj3, pool_k, pool_v, sproj3, sproj3, win_k, win_v)


def _merge_kernel(oc_ref, os_ref, ow_ref, gl_ref, za_ref, mgc_ref, mga_ref, cd_ref, x_ref, wa_ref, wo_ref, gp_ref,
                  y_ref):
    gates = _sigmoid(gl_ref[...])
    tm = gates.shape[0]
    pieces = []
    for h in range(N_HEADS):
        cols = slice(h * HEAD_DIM, (h + 1) * HEAD_DIM)
        gate = lambda c: jnp.broadcast_to(gates[:, c * N_HEADS + h:c * N_HEADS + h + 1], (tm, HEAD_DIM))
        pieces.append(gate(0) * oc_ref[:, cols] + gate(1) * os_ref[:, cols] + gate(2) * ow_ref[:, cols])
    o = jnp.concatenate(pieces, axis=1)
    attn_d = jnp.dot((o * _silu(za_ref[...])).astype(BF16), wa_ref[...], preferred_element_type=F32)
    mix = _sigmoid(mgc_ref[...]) * cd_ref[...] + _sigmoid(mga_ref[...]) * attn_d
    z = jnp.dot(mix.astype(BF16), wo_ref[...], preferred_element_type=F32)
    ms = jnp.mean(z * z, axis=-1, keepdims=True)
    y_ref[...] = x_ref[...] + z * lax.rsqrt(ms + RMS_EPS) * gp_ref[...]


def _merge(o_c, o_s, o_w, proj, conv_d, x2d, w_ao, w_o, g_post):
    m = x2d.shape[0]
    tm = min(128, m)
    wide = lambda col: pl.BlockSpec((tm, D_MODEL), lambda i: (i, col // D_MODEL))
    plain = pl.BlockSpec((tm, D_MODEL), lambda i: (i, 0))
    return pl.pallas_call(
        _merge_kernel,
        grid=(m // tm,),
        in_specs=[plain, plain, plain,
                  pl.BlockSpec((tm, LANE), lambda i: (i, C_GL // LANE)),
                  wide(C_ZA), wide(C_MGC), wide(C_MGA), plain, plain,
                  _const_spec((ATT_WIDTH, D_MODEL)), _const_spec((D_MODEL, D_MODEL)), _const_spec((1, D_MODEL))],
        out_specs=plain,
        out_shape=jax.ShapeDtypeStruct((m, D_MODEL), F32),
        compiler_params=_cparams(("parallel",)),
        name="merge",
    )(o_c, o_s, o_w, proj, proj, proj, proj, conv_d, x2d, w_ao, w_o, g_post.reshape(1, D_MODEL))


def _reorder_w_in(w_in):
    idx = [int(v) for v in np.cumsum(IN_SIZES)[:-1]]
    (u, ug, zc, q, kc, vc, ks, vs, kw, vw, za, gl, mgc, mga) = jnp.split(w_in, idx, axis=1)
    pad = jnp.zeros((D_MODEL, PROJ_W - (C_GL + 3 * N_HEADS)), w_in.dtype)
    return jnp.concatenate([u, ug, zc, q, za, mgc, mga, kc, vc, ks, vs, kw, vw, gl, pad], axis=1).astype(BF16)


def _alibi_slopes():
    h = jnp.arange(1, N_HEADS + 1, dtype=F32)
    return jnp.exp2(-8.0 * h / N_HEADS)


def _layer_weights(l, w_in, w_dw, w_conv_out, w_attn_out, w_out):
    w_dw_p = jnp.concatenate([w_dw[l], jnp.zeros((HALO - CONV_K, CONV_C), F32)], axis=0)
    return (_reorder_w_in(w_in[l]), w_dw_p, w_conv_out[l].astype(BF16), w_attn_out[l].astype(BF16),
            w_out[l].astype(BF16))


def _prompt_layer(x, slopes, w_r, w_dw_p, w_co, w_ao, w_o, g_pre, b_dw, ln_g, ln_b, pe_k, w1_k, w2_k, pe_v, w1_v,
                  w2_v, g_post):
    t = x.shape[1]
    x2d = x.reshape(t, D_MODEL)
    proj = _project(x2d, g_pre, w_r)
    conv_d, tail = _conv_branch(proj, w_dw_p, b_dw, ln_g, ln_b, w_co)
    kv = [proj[:, c:c + KV_WIDTH] for c in (C_KC, C_VC, C_KS, C_VS, C_KW, C_VW)]
    n_pages = t // PAGE_SIZE
    pt = jnp.arange(n_pages, dtype=jnp.int32).reshape(1, n_pages)
    view = lambda a: a.reshape(n_pages, SUB_PER_PAGE, PAGE_VIEW_W)
    kcs, vcs = _compress(pt, view(kv[0]), view(kv[1]), pe_k, w1_k, w2_k, pe_v, w1_v, w2_v, pps=min(16, n_pages))
    o_c, member = _prompt_attn1(slopes, proj, kcs[0], vcs[0])
    o_s = _prompt_flash(slopes, proj, member, _sel_expand_matrix(t), window=False)
    o_w = _prompt_flash(slopes, proj, None, None, window=True)
    y = _merge(o_c, o_s, o_w, proj, conv_d, x2d, w_ao, w_o, g_post)
    wl = min(WINDOW, t)
    heads = lambda a: a.reshape(1, -1, N_KV, HEAD_DIM)
    states = (heads(kv[0]), heads(kv[1]), heads(kv[2]), heads(kv[3]), heads(kv[4][-wl:]), heads(kv[5][-wl:]),
              tail[-(CONV_K - 1):].reshape(1, CONV_K - 1, CONV_C))
    return y.reshape(1, t, D_MODEL), states


def _sample_layer(x, l, ck_cmp, cv_cmp, ck_slc, cv_slc, ck_win, cv_win, c_conv, page_table, slopes, w_r, w_dw_p,
                  w_co, w_ao, w_o, g_pre, b_dw, ln_g, ln_b, pe_k, w1_k, w2_k, pe_v, w1_v, w2_v, g_post):
    b = x.shape[0]
    n_pool = ck_cmp.shape[1]
    n_pages = page_table.shape[1]
    past_len = n_pages * PAGE_SIZE
    w_buf = ck_win.shape[2]
    x2d = x.reshape(b, D_MODEL)
    sproj = _project(x2d, g_pre, w_r)
    sproj3 = sproj.reshape(b, 1, PROJ_W)
    conv_d, s_conv_t = _sample_conv(jnp.transpose(c_conv[l], (1, 0, 2)), sproj, w_dw_p, b_dw, ln_g, ln_b, w_co)
    s_conv = jnp.transpose(s_conv_t, (1, 0, 2))
    view = lambda a: a[l].reshape(n_pool, SUB_PER_PAGE, PAGE_VIEW_W)
    kcs, vcs = _compress(page_table, view(ck_cmp), view(cv_cmp), pe_k, w1_k, w2_k, pe_v, w1_v, w2_v,
                         pps=min(16, n_pages))
    o_c, idx = _sample_attn1(slopes, sproj3, kcs, vcs, past_len)
    pool = lambda a: a[l].reshape(n_pool, PAGE_SIZE, KV_WIDTH)
    win = lambda a: a[l].reshape(b, w_buf, KV_WIDTH)
    o_s, o_w, kwo, vwo = _sample_attn2(idx[:, :N_KV, :N_SEL], page_table, slopes, sproj3, pool(ck_slc), pool(cv_slc),
                                       win(ck_win), win(cv_win), past_len)
    y = _merge(o_c.reshape(b, ATT_WIDTH), o_s.reshape(b, ATT_WIDTH), o_w.reshape(b, ATT_WIDTH), sproj, conv_d, x2d,
               w_ao, w_o, g_post)
    heads = lambda a: a.reshape(b, -1, N_KV, HEAD_DIM)
    new = lambda c: sproj[:, c:c + KV_WIDTH].reshape(b, 1, N_KV, HEAD_DIM)
    states = (new(C_KC), new(C_VC), new(C_KS), new(C_VS), heads(kwo), heads(vwo), s_conv)
    return y.reshape(b, 1, D_MODEL), states


def kernel(x_prompt, x_sample, cache_k_cmp, cache_v_cmp, cache_k_slc, cache_v_slc, cache_k_win, cache_v_win,
           cache_conv, page_table, g_pre, w_in, w_dw, b_dw, ln_g, ln_b, w_conv_out, pe_k, w1_k, w2_k, pe_v, w1_v,
           w2_v, w_attn_out, w_out, g_post):
    depth = g_pre.shape[0]
    slopes = _alibi_slopes()
    y_prompt, y_sample = x_prompt, x_sample
    p_states, s_states = [], []
    for l in range(depth):
        mats = _layer_weights(l, w_in, w_dw, w_conv_out, w_attn_out, w_out)
        vecs = (g_pre[l], b_dw[l], ln_g[l], ln_b[l], pe_k[l], w1_k[l], w2_k[l], pe_v[l], w1_v[l], w2_v[l], g_post[l])
        y_prompt, sp = _prompt_layer(y_prompt, slopes, *mats, *vecs)
        y_sample, ss = _sample_layer(y_sample, l, cache_k_cmp, cache_v_cmp, cache_k_slc, cache_v_slc, cache_k_win,
                                     cache_v_win, cache_conv, page_table, slopes, *mats, *vecs)
        p_states.append(sp)
        s_states.append(ss)
    p_out = [jnp.stack(a) for a in zip(*p_states)]
    s_out = [jnp.stack(a) for a in zip(*s_states)]
    return (y_prompt, y_sample, *p_out, *s_out)
```

```python
import functools

import numpy as np
import jax
import jax.numpy as jnp
from jax import lax
from jax.experimental import pallas as pl
from jax.experimental.pallas import tpu as pltpu

F32 = jnp.float32
BF16 = jnp.bfloat16

D_MODEL = 2048
CONV_C = D_MODEL
CONV_K = 31
N_HEADS = 16
N_KV = 4
GROUP = N_HEADS // N_KV
HEAD_DIM = 128
ATT_WIDTH = N_HEADS * HEAD_DIM
KV_WIDTH = N_KV * HEAD_DIM
CMP_LEN = 32
CMP_STRIDE = 16
SEL_BLOCK = 64
N_SEL = 16
N_LOCAL = 2
WINDOW = 512
RMS_EPS = 1e-6
LN_EPS = 1e-5
FORCE_BONUS = 1e9
PAGE_SIZE = 128
IN_SIZES = (CONV_C, CONV_C, CONV_C, ATT_WIDTH) + (KV_WIDTH,) * 6 + (ATT_WIDTH, 3 * N_HEADS, D_MODEL, D_MODEL)

LANE = 128
SUBLANE = 8
NEG_INF = float("-inf")
SCALE = HEAD_DIM ** -0.5

C_U, C_UG, C_ZC, C_Q, C_ZA, C_MGC, C_MGA = (i * D_MODEL for i in range(7))
C_KC = 7 * D_MODEL
C_VC, C_KS, C_VS, C_KW, C_VW = (C_KC + i * KV_WIDTH for i in range(1, 6))
C_GL = C_KC + 6 * KV_WIDTH
PROJ_TN = 1280
PROJ_W = 14 * PROJ_TN
VMEM_LIMIT = 56 * 1024 * 1024


def _cparams(sem):
    return pltpu.CompilerParams(dimension_semantics=sem, vmem_limit_bytes=VMEM_LIMIT)


def _const_spec(shape):
    nd = len(shape)
    return pl.BlockSpec(shape, lambda *a: (0,) * nd, pipeline_mode=pl.Buffered(1))


def _sigmoid(x):
    return jax.nn.sigmoid(x)


def _silu(x):
    return x * jax.nn.sigmoid(x)


def _dot_t(a, b):
    return lax.dot_general(a, b, (((1,), (1,)), ((), ())), preferred_element_type=F32)


def _stack_rows(rows, n):
    w = rows[0].shape[1]
    rid = lax.broadcasted_iota(jnp.int32, (n, 1), 0)
    out = jnp.zeros((n, w), rows[0].dtype)
    for r, v in enumerate(rows):
        out = jnp.where(rid == r, jnp.broadcast_to(v, (n, w)), out)
    return out


def _take_row(x, r):
    rid = lax.broadcasted_iota(jnp.int32, (x.shape[0], 1), 0)
    return jnp.sum(jnp.where(rid == r, x, 0.0), axis=0, keepdims=True)


def _proj_kernel(x_ref, g_ref, w_ref, o_ref, h_ref):
    @pl.when(pl.program_id(1) == 0)
    def _():
        x = x_ref[...]
        ms = jnp.mean(x * x, axis=-1, keepdims=True)
        h_ref[...] = (x * lax.rsqrt(ms + RMS_EPS) * g_ref[...]).astype(BF16)

    o_ref[...] = jnp.dot(h_ref[...], w_ref[...], preferred_element_type=F32)


def _project(x2d, g_pre, w_r):
    m = x2d.shape[0]
    tm = min(512, m)
    return pl.pallas_call(
        _proj_kernel,
        grid=(m // tm, PROJ_W // PROJ_TN),
        in_specs=[
            pl.BlockSpec((tm, D_MODEL), lambda i, j: (i, 0)),
            pl.BlockSpec((1, D_MODEL), lambda i, j: (0, 0)),
            pl.BlockSpec((D_MODEL, PROJ_TN), lambda i, j: (0, j)),
        ],
        out_specs=pl.BlockSpec((tm, PROJ_TN), lambda i, j: (i, j)),
        out_shape=jax.ShapeDtypeStruct((m, PROJ_W), F32),
        scratch_shapes=[pltpu.VMEM((tm, D_MODEL), BF16)],
        compiler_params=_cparams(("parallel", "arbitrary")),
        name="proj",
    )(x2d, g_pre.reshape(1, D_MODEL), w_r)


HALO = 32
CONV_RB = 8


def _ln_gate_out(c, zc, bdw, lng, lnb, wco_ref):
    c = c + bdw
    mean = jnp.mean(c, axis=-1, keepdims=True)
    xc = c - mean
    var = jnp.mean(xc * xc, axis=-1, keepdims=True)
    y = xc * lax.rsqrt(var + LN_EPS) * lng + lnb
    act = _silu(y) * _silu(zc)
    return jnp.dot(act.astype(BF16), wco_ref[...], preferred_element_type=F32)


def _conv_kernel(u_ref, ug_ref, zc_ref, uh_ref, ugh_ref, wdw_ref, bdw_ref, lng_ref, lnb_ref, wco_ref,
                 out_ref, tail_ref, uext_ref, sh_ref, c_ref):
    i = pl.program_id(0)
    tm = u_ref.shape[0]
    glu = u_ref[...] * _sigmoid(ug_ref[...])
    halo = uh_ref[...] * _sigmoid(ugh_ref[...])
    not_first = (jnp.zeros((HALO, 1), jnp.int32) + i) > 0
    uext_ref[0:HALO, :] = jnp.where(not_first, halo, 0.0)
    uext_ref[HALO:HALO + tm, :] = glu
    tail_ref[...] = glu[tm - HALO:, :]
    n_sh = tm + HALO - SUBLANE
    for p in range(1, SUBLANE):
        sh_ref[p - 1] = uext_ref[p:p + n_sh, :]

    off = HALO - (CONV_K - 1)

    def body(rc, carry):
        r0 = pl.multiple_of(rc * CONV_RB, CONV_RB)
        acc = jnp.zeros((CONV_RB, CONV_C), F32)
        for k in range(CONV_K):
            p, base = (off + k) % SUBLANE, (off + k) // SUBLANE * SUBLANE
            if p == 0:
                win = uext_ref[pl.ds(r0 + base, CONV_RB), :]
            else:
                win = sh_ref[p - 1, pl.ds(r0 + base, CONV_RB), :]
            acc = acc + win * wdw_ref[k]
        c_ref[pl.ds(r0, CONV_RB), :] = acc
        return carry

    lax.fori_loop(0, tm // CONV_RB, body, 0)
    out_ref[...] = _ln_gate_out(c_ref[...], zc_ref[...], bdw_ref[...], lng_ref[...], lnb_ref[...], wco_ref)


def _conv_branch(proj, w_dw_p, b_dw, ln_g, ln_b, w_co):
    t = proj.shape[0]
    tm = min(256, t)
    hb = tm // HALO
    row = lambda a: a.reshape(1, CONV_C)
    return pl.pallas_call(
        _conv_kernel,
        grid=(t // tm,),
        in_specs=[
            pl.BlockSpec((tm, CONV_C), lambda i: (i, C_U // CONV_C)),
            pl.BlockSpec((tm, CONV_C), lambda i: (i, C_UG // CONV_C)),
            pl.BlockSpec((tm, CONV_C), lambda i: (i, C_ZC // CONV_C)),
            pl.BlockSpec((HALO, CONV_C), lambda i: (jnp.maximum(i * hb - 1, 0), C_U // CONV_C)),
            pl.BlockSpec((HALO, CONV_C), lambda i: (jnp.maximum(i * hb - 1, 0), C_UG // CONV_C)),
            _const_spec((CONV_K, CONV_RB, CONV_C)),
            _const_spec((1, CONV_C)),
            _const_spec((1, CONV_C)),
            _const_spec((1, CONV_C)),
            _const_spec((CONV_C, D_MODEL)),
        ],
        out_specs=[
            pl.BlockSpec((tm, D_MODEL), lambda i: (i, 0)),
            pl.BlockSpec((HALO, CONV_C), lambda i: (0, 0)),
        ],
        out_shape=[jax.ShapeDtypeStruct((t, D_MODEL), F32), jax.ShapeDtypeStruct((HALO, CONV_C), F32)],
        scratch_shapes=[pltpu.VMEM((tm + HALO, CONV_C), F32),
                        pltpu.VMEM((SUBLANE - 1, tm + HALO - SUBLANE, CONV_C), F32),
                        pltpu.VMEM((tm, CONV_C), F32)],
        compiler_params=_cparams(("arbitrary",)),
        name="conv_branch",
    )(proj, proj, proj, proj, proj,
      jnp.broadcast_to(w_dw_p[:CONV_K, None, :], (CONV_K, CONV_RB, CONV_C)),
      row(b_dw), row(ln_g), row(ln_b), w_co)


def _sconv_kernel(cc_ref, u_ref, ug_ref, zc_ref, wdw_ref, bdw_ref, lng_ref, lnb_ref, wco_ref,
                  out_ref, state_ref):
    nh = CONV_K - 1
    glu = u_ref[...] * _sigmoid(ug_ref[...])
    c = glu * wdw_ref[nh:nh + 1, :]
    for k in range(nh):
        c = c + cc_ref[k] * wdw_ref[k:k + 1, :]
    out_ref[...] = _ln_gate_out(c, zc_ref[...], bdw_ref[...], lng_ref[...], lnb_ref[...], wco_ref)
    for k in range(nh - 1):
        state_ref[k] = cc_ref[k + 1]
    state_ref[nh - 1] = glu


def _sample_conv(cc, sproj, w_dw_p, b_dw, ln_g, ln_b, w_co):
    nh, b = cc.shape[0], cc.shape[1]
    row = lambda a: a.reshape(1, CONV_C)
    return pl.pallas_call(
        _sconv_kernel,
        grid=(1,),
        in_specs=[
            pl.BlockSpec((nh, b, CONV_C), lambda i: (0, 0, 0)),
            pl.BlockSpec((b, CONV_C), lambda i: (0, C_U // CONV_C)),
            pl.BlockSpec((b, CONV_C), lambda i: (0, C_UG // CONV_C)),
            pl.BlockSpec((b, CONV_C), lambda i: (0, C_ZC // CONV_C)),
            pl.BlockSpec((HALO, CONV_C), lambda i: (0, 0)),
            pl.BlockSpec((1, CONV_C), lambda i: (0, 0)),
            pl.BlockSpec((1, CONV_C), lambda i: (0, 0)),
            pl.BlockSpec((1, CONV_C), lambda i: (0, 0)),
            pl.BlockSpec((CONV_C, D_MODEL), lambda i: (0, 0)),
        ],
        out_specs=[
            pl.BlockSpec((b, D_MODEL), lambda i: (0, 0)),
            pl.BlockSpec((nh, b, CONV_C), lambda i: (0, 0, 0)),
        ],
        out_shape=[jax.ShapeDtypeStruct((b, D_MODEL), F32), jax.ShapeDtypeStruct((nh, b, CONV_C), F32)],
        compiler_params=_cparams(("arbitrary",)),
        name="sample_conv",
    )(cc, sproj, sproj, sproj, w_dw_p, row(b_dw), row(ln_g), row(ln_b), w_co)


SUB_PER_PAGE = PAGE_SIZE // CMP_STRIDE
PAGE_ROWS = PAGE_SIZE * N_KV
R_N = CMP_LEN // CMP_STRIDE
STEP_PERM = tuple(range(0, CMP_STRIDE, 2)) + tuple(range(1, CMP_STRIDE, 2))
HALF_SUB = SUBLANE // 2


def _compress_kernel(pt_ref, *refs, pps):
    del pt_ref
    pages = (refs[:pps], refs[pps:2 * pps])
    w01s = refs[2 * pps:2 * pps + 2]
    pes = refs[2 * pps + 2:2 * pps + 4]
    w2s = refs[2 * pps + 4:2 * pps + 6]
    outs = refs[2 * pps + 6:2 * pps + 8]
    carries = refs[2 * pps + 8:2 * pps + 10]
    xs_ref = refs[2 * pps + 10]
    rows = pps * SUB_PER_PAGE * N_KV
    n_c = CMP_STRIDE // 2
    grp = n_c * SUBLANE

    @pl.when(pl.program_id(1) == 0)
    def _():
        carries[0][...] = jnp.zeros_like(carries[0])
        carries[1][...] = jnp.zeros_like(carries[1])

    top3 = lax.broadcasted_iota(jnp.int32, (n_c, SUBLANE, HEAD_DIM), 1) < HALF_SUB
    top = lax.broadcasted_iota(jnp.int32, (SUBLANE, HEAD_DIM), 0) < HALF_SUB
    for which in range(2):
        for i, pg in enumerate(pages[which]):
            los, his = [], []
            for p in range(SUB_PER_PAGE // 2):
                a = pg[(2 * p) * grp:(2 * p + 1) * grp, :].reshape(n_c, SUBLANE, HEAD_DIM)
                b = pg[(2 * p + 1) * grp:(2 * p + 2) * grp, :].reshape(n_c, SUBLANE, HEAD_DIM)
                los.append(jnp.where(top3, a, pltpu.roll(b, HALF_SUB, 1)))
                his.append(jnp.where(top3, pltpu.roll(a, HALF_SUB, 1), b))
            for pp in range(len(los) // 2):
                r0 = (i * (SUB_PER_PAGE // 2) + 2 * pp) * SUBLANE
                for c in range(n_c):
                    for half, parts in ((0, los), (1, his)):
                        col = (half * n_c + c) * HEAD_DIM
                        xs_ref[r0:r0 + 2 * SUBLANE, col:col + HEAD_DIM] = jnp.concatenate(
                            [parts[2 * pp][c], parts[2 * pp + 1][c]], axis=0).astype(BF16)
        w01 = w01s[which][...]
        a = jnp.dot(xs_ref[...], w01, preferred_element_type=F32)
        pe_o = jnp.dot(pes[which][...].astype(BF16), w01, preferred_element_type=F32)
        hid0 = pe_o[0:1, :HEAD_DIM] + pe_o[1:2, HEAD_DIM:]
        carry = carries[which]
        a0, a1 = a[:, :HEAD_DIM], a[:, HEAD_DIM:]
        a0s = pltpu.roll(a0, N_KV, 0)
        first = jnp.where(top, carry[...], a0s[0:SUBLANE])
        a0s = jnp.concatenate([first, a0s[SUBLANE:]], axis=0)
        carry[...] = pltpu.roll(a0[rows - SUBLANE:rows, :], N_KV, 0)
        hid = hid0 + a0s + a1
        outs[which][...] = jnp.dot(_silu(hid).astype(BF16), w2s[which][...], preferred_element_type=F32)


def _w01(w1):
    w = w1.reshape(R_N, CMP_STRIDE, HEAD_DIM, HEAD_DIM)[:, STEP_PERM, :, :]
    w = w.reshape(R_N, CMP_STRIDE * HEAD_DIM, HEAD_DIM)
    return jnp.concatenate([w[r] for r in range(R_N)], axis=1).astype(BF16)


def _pe2(pe):
    p = pe.reshape(R_N, CMP_STRIDE, HEAD_DIM)[:, STEP_PERM, :].reshape(R_N, CMP_STRIDE * HEAD_DIM)
    return jnp.concatenate([p, jnp.zeros((SUBLANE - R_N, CMP_STRIDE * HEAD_DIM), F32)], axis=0)


def _compress(page_table, cache_k, cache_v, pe_k, w1_k, w2_k, pe_v, w1_v, w2_v, pps):
    b, n_pages = page_table.shape
    rows = pps * SUB_PER_PAGE * N_KV
    n_sub = n_pages * SUB_PER_PAGE

    def page_spec(i):
        return pl.BlockSpec((PAGE_ROWS, HEAD_DIM), lambda bb, c, pt: (pt[bb, c * pps + i], 0))

    cspec = lambda shape: pl.BlockSpec(shape, lambda bb, c, pt: (0,) * len(shape))
    in_specs = [page_spec(i) for i in range(pps)] * 2 + [
        cspec((CMP_STRIDE * HEAD_DIM, R_N * HEAD_DIM))] * 2 + [
        cspec((SUBLANE, CMP_STRIDE * HEAD_DIM))] * 2 + [cspec((HEAD_DIM, HEAD_DIM))] * 2
    out_spec = pl.BlockSpec((None, rows, HEAD_DIM), lambda bb, c, pt: (bb, c, 0))
    return pl.pallas_call(
        functools.partial(_compress_kernel, pps=pps),
        grid_spec=pltpu.PrefetchScalarGridSpec(
            num_scalar_prefetch=1,
            grid=(b, n_pages // pps),
            in_specs=in_specs,
            out_specs=[out_spec, out_spec],
            scratch_shapes=[pltpu.VMEM((SUBLANE, HEAD_DIM), F32)] * 2
            + [pltpu.VMEM((rows, CMP_STRIDE * HEAD_DIM), BF16)],
        ),
        out_shape=[jax.ShapeDtypeStruct((b, n_sub * N_KV, HEAD_DIM), F32)] * 2,
        compiler_params=_cparams(("parallel", "arbitrary")),
        name="compress",
    )(page_table, *([cache_k] * pps), *([cache_v] * pps), _w01(w1_k), _w01(w1_v), _pe2(pe_k), _pe2(pe_v),
      w2_k.astype(BF16), w2_v.astype(BF16))


def _ovl_shifted(n_sub, n_lanes):
    i = (np.arange(n_sub)[:, None] - 1) * CMP_STRIDE
    j = np.arange(n_lanes)[None, :] * SEL_BLOCK
    ov = np.clip(np.minimum(i + CMP_LEN, j + SEL_BLOCK) - np.maximum(i, j), 0, None).astype(np.float32) / CMP_LEN
    ov[0, :] = 0.0
    return jnp.asarray(ov, dtype=BF16)


def _softmax_rows(s, valid):
    s = jnp.where(valid, s, NEG_INF)
    mx = jnp.max(s, axis=-1, keepdims=True)
    mx = jnp.where(mx == NEG_INF, 0.0, mx)
    p = jnp.exp(s - mx)
    den = jnp.maximum(jnp.sum(p, axis=-1, keepdims=True), 1e-30)
    return p * (1.0 / den)


def _select_blocks(imp, qpos, n_sel_blocks):
    r, l = imp.shape
    lane = lax.broadcasted_iota(jnp.int32, (r, l), 1)
    lanef = lane.astype(F32)
    cur = lax.shift_right_logical(qpos, int(np.log2(SEL_BLOCK)))
    valid = lane * SEL_BLOCK <= qpos
    forced = (lane == 0) | ((lane <= cur) & (lane > cur - N_LOCAL))
    score = jnp.where(valid, imp + jnp.where(forced, FORCE_BONUS, 0.0), -FORCE_BONUS)
    score = jnp.where(lane < n_sel_blocks, score, -3e38)
    member = jnp.zeros((r, l), F32)
    picks = []
    for _ in range(N_SEL):
        mx = jnp.max(score, axis=-1, keepdims=True)
        idx = jnp.min(jnp.where(score == mx, lanef, float(l)), axis=-1, keepdims=True)
        hit = lanef == idx
        member = jnp.where(hit, 1.0, member)
        score = jnp.where(hit, NEG_INF, score)
        picks.append(idx)
    return picks, member


def _attn1_kernel(sl_ref, q_ref, kc_ref, vc_ref, ovl_ref, oc_ref, mem_ref, *, tq, n_sub, n_sel_blocks):
    qi = pl.program_id(0)
    qpos = qi * tq + lax.broadcasted_iota(jnp.int32, (tq, 1), 0)
    m = lax.broadcasted_iota(jnp.int32, (1, n_sub), 1)
    c_end = (m - 1) * CMP_STRIDE + (CMP_LEN - 1)
    c_ctr = ((m - 1) * CMP_STRIDE).astype(F32) + (CMP_LEN - 1) / 2
    valid = (m >= 1) & (c_end <= qpos)
    dist = jnp.abs(qpos.astype(F32) - c_ctr)
    for g in range(N_KV):
        qg = jnp.concatenate([q_ref[:, (g * GROUP + r) * HEAD_DIM:(g * GROUP + r + 1) * HEAD_DIM]
                              for r in range(GROUP)], axis=0).astype(BF16)
        kg = kc_ref[pl.ds(g, n_sub, stride=N_KV), :].astype(BF16)
        s = _dot_t(qg, kg)
        ps = []
        for r in range(GROUP):
            sr = s[r * tq:(r + 1) * tq] * SCALE - sl_ref[g * GROUP + r] * dist
            ps.append(_softmax_rows(sr, valid))
        psum = ps[0]
        for r in range(1, GROUP):
            psum = psum + ps[r]
        pcat = jnp.concatenate(ps, axis=0).astype(BF16)
        oc = jnp.dot(pcat, vc_ref[pl.ds(g, n_sub, stride=N_KV), :].astype(BF16), preferred_element_type=F32)
        for r in range(GROUP):
            oc_ref[:, (g * GROUP + r) * HEAD_DIM:(g * GROUP + r + 1) * HEAD_DIM] = oc[r * tq:(r + 1) * tq]
        imp = jnp.dot(psum.astype(BF16), ovl_ref[...], preferred_element_type=F32)
        _, member = _select_blocks(imp, qpos, n_sel_blocks)
        mem_ref[g] = member.astype(BF16)


def _prompt_attn1(slopes, proj, kcs, vcs):
    t = proj.shape[0]
    n_sub = kcs.shape[0] // N_KV
    tq = min(128, t)
    n_sel_blocks = -(-t // SEL_BLOCK)
    assert n_sel_blocks <= LANE
    ovl = _ovl_shifted(n_sub, LANE)
    return pl.pallas_call(
        functools.partial(_attn1_kernel, tq=tq, n_sub=n_sub, n_sel_blocks=n_sel_blocks),
        grid_spec=pltpu.PrefetchScalarGridSpec(
            num_scalar_prefetch=1,
            grid=(t // tq,),
            in_specs=[
                pl.BlockSpec((tq, ATT_WIDTH), lambda i, sl: (i, C_Q // ATT_WIDTH)),
                pl.BlockSpec((n_sub * N_KV, HEAD_DIM), lambda i, sl: (0, 0)),
                pl.BlockSpec((n_sub * N_KV, HEAD_DIM), lambda i, sl: (0, 0)),
                pl.BlockSpec((n_sub, LANE), lambda i, sl: (0, 0)),
            ],
            out_specs=[
                pl.BlockSpec((tq, ATT_WIDTH), lambda i, sl: (i, 0)),
                pl.BlockSpec((N_KV, tq, LANE), lambda i, sl: (0, i, 0)),
            ],
        ),
        out_shape=[jax.ShapeDtypeStruct((t, ATT_WIDTH), F32), jax.ShapeDtypeStruct((N_KV, t, LANE), BF16)],
        compiler_params=_cparams(("parallel",)),
        name="prompt_attn_cmp",
    )(slopes, proj, kcs, vcs, ovl)


LOG2E = float(np.log2(np.e))


def _flash_kernel(sl_ref, qt_ref, kt_ref, fl_ref, *refs, tq, tk, window):
    if window:
        q_ref, k_ref, v_ref, o_ref, qs_ref, m_ref, l_ref, acc_ref = refs
    else:
        q_ref, k_ref, v_ref, mem_ref, et_ref, o_ref, qs_ref, m_ref, l_ref, acc_ref = refs
    g = pl.program_id(0)
    step = pl.program_id(1)
    qi = qt_ref[step]
    kt = kt_ref[step]
    flags = fl_ref[step]

    @pl.when((flags & 1) != 0)
    def _():
        for r in range(GROUP):
            qs_ref[r * tq:(r + 1) * tq, :] = (q_ref[:, r * HEAD_DIM:(r + 1) * HEAD_DIM] * (SCALE * LOG2E)).astype(BF16)
        m_ref[...] = jnp.full(m_ref.shape, NEG_INF, F32)
        l_ref[...] = jnp.zeros(l_ref.shape, F32)
        acc_ref[...] = jnp.zeros(acc_ref.shape, F32)

    st = _dot_t(k_ref[...].astype(BF16), qs_ref[...])
    kpos = kt * tk + lax.broadcasted_iota(jnp.int32, (tk, 1), 0)
    qpos = qi * tq + lax.broadcasted_iota(jnp.int32, (1, tq), 1)
    rel = kpos - qpos
    if window:
        mask = (rel <= 0) & (rel > -WINDOW)
    else:
        mask = (_dot_t(et_ref[...], mem_ref[...]) > 0.5) & (rel <= 0)
    koff = (kpos - qi * tq).astype(F32)
    vt = v_ref[...].T.astype(BF16)
    ps = []
    for r in range(GROUP):
        cols = slice(r * tq, (r + 1) * tq)
        sr = jnp.where(mask, st[:, cols] + (sl_ref[g * GROUP + r] * LOG2E) * koff, NEG_INF)
        m_prev = m_ref[:, cols]
        m_new = jnp.maximum(m_prev, jnp.max(sr, axis=0, keepdims=True))
        m_safe = jnp.where(m_new == NEG_INF, 0.0, m_new)
        p = jnp.exp2(sr - m_safe)
        alpha = jnp.exp2(m_prev - m_safe)
        l_ref[:, cols] = alpha * l_ref[:, cols] + jnp.sum(p, axis=0, keepdims=True)
        acc_ref[:, cols] = alpha * acc_ref[:, cols]
        m_ref[:, cols] = m_new
        ps.append(p.astype(BF16))
    acc_ref[...] += jnp.dot(vt, jnp.concatenate(ps, axis=1), preferred_element_type=F32)

    @pl.when((flags & 2) != 0)
    def _():
        for r in range(GROUP):
            cols = slice(r * tq, (r + 1) * tq)
            o = acc_ref[:, cols] * (1.0 / jnp.maximum(l_ref[:, cols], 1e-30))
            o_ref[:, r * HEAD_DIM:(r + 1) * HEAD_DIM] = o.T


def _flash_steps(t, tq, tk, window):
    qt, kt, fl = [], [], []
    for qi in range(t // tq):
        q0, q1 = qi * tq, qi * tq + tq - 1
        lo = max(0, (q0 - WINDOW + 1) // tk) if window else 0
        hi = q1 // tk
        for k in range(lo, hi + 1):
            qt.append(qi)
            kt.append(k)
            fl.append((1 if k == lo else 0) | (2 if k == hi else 0))
    as_i32 = lambda a: jnp.asarray(np.asarray(a, np.int32))
    return as_i32(qt), as_i32(kt), as_i32(fl)


def _prompt_flash(slopes, proj, member, emat_t, window):
    t = proj.shape[0]
    if window:
        tq = tk = min(256, t)
        ck, cv = C_KW, C_VW
    else:
        tq, tk = min(256, t), min(1024, t)
        ck, cv = C_KS, C_VS
    qt, kt, fl = _flash_steps(t, tq, tk, window)
    in_specs = [
        pl.BlockSpec((tq, GROUP * HEAD_DIM), lambda g, s, sl, qt, kt, fl: (qt[s], C_Q // (GROUP * HEAD_DIM) + g)),
        pl.BlockSpec((tk, HEAD_DIM), lambda g, s, sl, qt, kt, fl: (kt[s], ck // HEAD_DIM + g)),
        pl.BlockSpec((tk, HEAD_DIM), lambda g, s, sl, qt, kt, fl: (kt[s], cv // HEAD_DIM + g)),
    ]
    args = [proj, proj, proj]
    if not window:
        in_specs += [
            pl.BlockSpec((None, tq, LANE), lambda g, s, sl, qt, kt, fl: (g, qt[s], 0)),
            pl.BlockSpec((tk, LANE), lambda g, s, sl, qt, kt, fl: (kt[s], 0)),
        ]
        args += [member, emat_t]
    return pl.pallas_call(
        functools.partial(_flash_kernel, tq=tq, tk=tk, window=window),
        grid_spec=pltpu.PrefetchScalarGridSpec(
            num_scalar_prefetch=4,
            grid=(N_KV, int(qt.shape[0])),
            in_specs=in_specs,
            out_specs=pl.BlockSpec((tq, GROUP * HEAD_DIM), lambda g, s, sl, qt, kt, fl: (qt[s], g)),
            scratch_shapes=[
                pltpu.VMEM((GROUP * tq, HEAD_DIM), BF16),
                pltpu.VMEM((1, GROUP * tq), F32),
                pltpu.VMEM((1, GROUP * tq), F32),
                pltpu.VMEM((HEAD_DIM, GROUP * tq), F32),
            ],
        ),
        out_shape=jax.ShapeDtypeStruct((t, ATT_WIDTH), F32),
        compiler_params=_cparams(("parallel", "arbitrary")),
        name="prompt_attn_win" if window else "prompt_attn_sel",
    )(slopes, qt, kt, fl, *args)


def _sel_expand_matrix_t(t):
    e = (np.arange(t)[:, None] // SEL_BLOCK) == np.arange(LANE)[None, :]
    return jnp.asarray(e, dtype=BF16)


def _sattn1_kernel(sl_ref, q_ref, kc_ref, vc_ref, ovl_ref, oc_ref, idx_ref, *, n_sub, qpos, n_sel_blocks, n_lanes):
    m = lax.broadcasted_iota(jnp.int32, (1, n_sub), 1)
    c_end = (m - 1) * CMP_STRIDE + (CMP_LEN - 1)
    c_ctr = ((m - 1) * CMP_STRIDE).astype(F32) + (CMP_LEN - 1) / 2
    valid = (m >= 1) & (c_end <= qpos)
    dist = jnp.abs(float(qpos) - c_ctr)
    qposv = jnp.full((SUBLANE, 1), qpos, jnp.int32)
    out_lane = lax.broadcasted_iota(jnp.int32, (SUBLANE, LANE), 1)
    out_row = lax.broadcasted_iota(jnp.int32, (SUBLANE, LANE), 0)
    idx_acc = jnp.zeros((SUBLANE, LANE), F32)
    head_row = lax.broadcasted_iota(jnp.int32, (SUBLANE, 1), 0)
    for g in range(N_KV):
        qg = _stack_rows([q_ref[:, (g * GROUP + r) * HEAD_DIM:(g * GROUP + r + 1) * HEAD_DIM]
                          for r in range(GROUP)], SUBLANE).astype(BF16)
        s = _dot_t(qg, kc_ref[pl.ds(g, n_sub, stride=N_KV), :].astype(BF16))
        slope = jnp.zeros((SUBLANE, 1), F32)
        for r in range(GROUP):
            slope = jnp.where(head_row == r, sl_ref[g * GROUP + r], slope)
        p = _softmax_rows(s * SCALE - slope * dist, valid)
        p = jnp.where(head_row < GROUP, p, 0.0)
        oc = jnp.dot(p.astype(BF16), vc_ref[pl.ds(g, n_sub, stride=N_KV), :].astype(BF16),
                     preferred_element_type=F32)
        for r in range(GROUP):
            oc_ref[:, (g * GROUP + r) * HEAD_DIM:(g * GROUP + r + 1) * HEAD_DIM] = _take_row(oc, r)
        psum = jnp.sum(p, axis=0, keepdims=True)
        psum8 = jnp.broadcast_to(psum, (SUBLANE, n_sub))
        imp = jnp.dot(psum8.astype(BF16), ovl_ref[...], preferred_element_type=F32)
        picks, _ = _select_blocks(imp, qposv, n_sel_blocks)
        for j, pk in enumerate(picks):
            idx_acc = jnp.where((out_row == g) & (out_lane == j), pk, idx_acc)
    idx_ref[...] = idx_acc.astype(jnp.int32)


def _sample_attn1(slopes, sproj3, kcs, vcs, past_len):
    b = sproj3.shape[0]
    n_sub = kcs.shape[1] // N_KV
    n_sel_blocks = -(-(past_len + 1) // SEL_BLOCK)
    n_lanes = -(-n_sel_blocks // LANE) * LANE
    ovl = _ovl_shifted(n_sub, n_lanes)
    return pl.pallas_call(
        functools.partial(_sattn1_kernel, n_sub=n_sub, qpos=past_len, n_sel_blocks=n_sel_blocks, n_lanes=n_lanes),
        grid_spec=pltpu.PrefetchScalarGridSpec(
            num_scalar_prefetch=1,
            grid=(b,),
            in_specs=[
                pl.BlockSpec((None, 1, ATT_WIDTH), lambda i, sl: (i, 0, C_Q // ATT_WIDTH)),
                pl.BlockSpec((None, n_sub * N_KV, HEAD_DIM), lambda i, sl: (i, 0, 0)),
                pl.BlockSpec((None, n_sub * N_KV, HEAD_DIM), lambda i, sl: (i, 0, 0)),
                pl.BlockSpec((n_sub, n_lanes), lambda i, sl: (0, 0)),
            ],
            out_specs=[
                pl.BlockSpec((None, 1, ATT_WIDTH), lambda i, sl: (i, 0, 0)),
                pl.BlockSpec((None, SUBLANE, LANE), lambda i, sl: (i, 0, 0)),
            ],
        ),
        out_shape=[jax.ShapeDtypeStruct((b, 1, ATT_WIDTH), F32), jax.ShapeDtypeStruct((b, SUBLANE, LANE), jnp.int32)],
        compiler_params=_cparams(("parallel",)),
        name="sample_attn_cmp",
    )(slopes, sproj3, kcs, vcs, ovl)


SEL_PER_STEP = 4
BLK_ROWS = SEL_BLOCK * N_KV


def _sattn2_kernel(idx_ref, pt_ref, sl_ref, q_ref, ksn_ref, vsn_ref, kwn_ref, vwn_ref, *refs, past_len, w_buf):
    del pt_ref
    nblk = N_KV * SEL_PER_STEP
    kbs, vbs = refs[:nblk], refs[nblk:2 * nblk]
    kwc_ref, vwc_ref, os_ref, ow_ref, kwo_ref, vwo_ref, qs_ref, m_ref, l_ref, acc_ref = refs[2 * nblk:]
    b = pl.program_id(0)
    jt = pl.program_id(1)
    head_row = lax.broadcasted_iota(jnp.int32, (SUBLANE, 1), 0)

    def slope_col(g):
        slope = jnp.zeros((SUBLANE, 1), F32)
        for r in range(GROUP):
            slope = jnp.where(head_row == r, sl_ref[g * GROUP + r], slope)
        return slope

    @pl.when(jt == 0)
    def _():
        for g in range(N_KV):
            qs_ref[g * SUBLANE:(g + 1) * SUBLANE, :] = _stack_rows(
                [q_ref[:, (g * GROUP + r) * HEAD_DIM:(g * GROUP + r + 1) * HEAD_DIM] for r in range(GROUP)],
                SUBLANE)
        m_ref[...] = jnp.full(m_ref.shape, NEG_INF, F32)
        l_ref[...] = jnp.zeros(l_ref.shape, F32)
        acc_ref[...] = jnp.zeros(acc_ref.shape, F32)

    n_keys = SEL_PER_STEP * SEL_BLOCK
    lane = lax.broadcasted_iota(jnp.int32, (1, n_keys), 1)
    pos_in_blk = lax.broadcasted_iota(jnp.int32, (SEL_BLOCK, 1), 0)
    for g in range(N_KV):
        rows = slice(g * SUBLANE, (g + 1) * SUBLANE)
        kparts, vparts = [], []
        blkv = jnp.zeros((1, n_keys), jnp.int32)
        for jj in range(SEL_PER_STEP):
            blk = idx_ref[b, g, jt * SEL_PER_STEP + jj]
            is_past = (blk * SEL_BLOCK + pos_in_blk) < past_len
            new_k = jnp.broadcast_to(ksn_ref[:, g * HEAD_DIM:(g + 1) * HEAD_DIM], (SEL_BLOCK, HEAD_DIM))
            new_v = jnp.broadcast_to(vsn_ref[:, g * HEAD_DIM:(g + 1) * HEAD_DIM], (SEL_BLOCK, HEAD_DIM))
            kparts.append(jnp.where(is_past, kbs[g * SEL_PER_STEP + jj][pl.ds(g, SEL_BLOCK, stride=N_KV), :], new_k))
            vparts.append(jnp.where(is_past, vbs[g * SEL_PER_STEP + jj][pl.ds(g, SEL_BLOCK, stride=N_KV), :], new_v))
            blkv = jnp.where(lax.shift_right_logical(lane, int(np.log2(SEL_BLOCK))) == jj, blk, blkv)
        kcat = jnp.concatenate(kparts, axis=0).astype(BF16)
        vcat = jnp.concatenate(vparts, axis=0).astype(BF16)
        rel = blkv * SEL_BLOCK + (lane & (SEL_BLOCK - 1)) - past_len
        s = _dot_t(qs_ref[rows, :].astype(BF16), kcat) * SCALE + slope_col(g) * rel.astype(F32)
        s = jnp.where(rel <= 0, s, NEG_INF)
        m_prev = m_ref[rows]
        m_new = jnp.maximum(m_prev, jnp.max(s, axis=-1, keepdims=True))
        m_safe = jnp.where(m_new == NEG_INF, 0.0, m_new)
        p = jnp.exp(s - m_safe)
        alpha = jnp.exp(m_prev - m_safe)
        l_ref[rows] = alpha * l_ref[rows] + jnp.sum(p, axis=-1, keepdims=True)
        acc_ref[rows] = alpha * acc_ref[rows] + jnp.dot(p.astype(BF16), vcat, preferred_element_type=F32)
        m_ref[rows] = m_new

    @pl.when(jt == N_SEL // SEL_PER_STEP - 1)
    def _():
        i = lax.broadcasted_iota(jnp.int32, (1, w_buf + SUBLANE), 1)
        relw = jnp.minimum(i, w_buf) - w_buf
        validw = (relw > -WINDOW) & (i <= w_buf)
        for g in range(N_KV):
            rows = slice(g * SUBLANE, (g + 1) * SUBLANE)
            o = acc_ref[rows] * (1.0 / jnp.maximum(l_ref[rows], 1e-30))
            new_k = jnp.broadcast_to(kwn_ref[:, g * HEAD_DIM:(g + 1) * HEAD_DIM], (SUBLANE, HEAD_DIM))
            new_v = jnp.broadcast_to(vwn_ref[:, g * HEAD_DIM:(g + 1) * HEAD_DIM], (SUBLANE, HEAD_DIM))
            kw = jnp.concatenate([kwc_ref[pl.ds(g, w_buf, stride=N_KV), :], new_k], axis=0).astype(BF16)
            vw = jnp.concatenate([vwc_ref[pl.ds(g, w_buf, stride=N_KV), :], new_v], axis=0).astype(BF16)
            sw = _dot_t(qs_ref[rows, :].astype(BF16), kw) * SCALE + slope_col(g) * relw.astype(F32)
            ow = jnp.dot(_softmax_rows(sw, validw).astype(BF16), vw, preferred_element_type=F32)
            for r in range(GROUP):
                cols = slice((g * GROUP + r) * HEAD_DIM, (g * GROUP + r + 1) * HEAD_DIM)
                os_ref[:, cols] = _take_row(o, r)
                ow_ref[:, cols] = _take_row(ow, r)
        n_rows = w_buf * N_KV
        rowi = lax.broadcasted_iota(jnp.int32, (n_rows, HEAD_DIM), 0)
        for src, new, dst in ((kwc_ref, kwn_ref, kwo_ref), (vwc_ref, vwn_ref, vwo_ref)):
            out = pltpu.roll(src[...], n_rows - N_KV, 0)
            for g in range(N_KV):
                out = jnp.where(rowi == n_rows - N_KV + g, new[:, g * HEAD_DIM:(g + 1) * HEAD_DIM], out)
            dst[...] = out


def _sample_attn2(idx, page_table, slopes, sproj3, pool_k, pool_v, win_k, win_v, past_len):
    b = sproj3.shape[0]
    w_buf = win_k.shape[0] // (b * N_KV)
    n_blk = past_len // SEL_BLOCK
    bpp = PAGE_SIZE // SEL_BLOCK

    def blk_spec(g, jj):
        def index_map(bb, jt, ix, pt, sl):
            blk = jnp.minimum(ix[bb, g, jt * SEL_PER_STEP + jj], n_blk - 1)
            return (pt[bb, blk // bpp] * bpp + blk % bpp, 0)
        return pl.BlockSpec((BLK_ROWS, HEAD_DIM), index_map)

    blk_specs = [blk_spec(g, jj) for g in range(N_KV) for jj in range(SEL_PER_STEP)]
    row = lambda col, w: pl.BlockSpec((None, 1, w), lambda bb, jt, ix, pt, sl: (bb, 0, col // w))
    win_spec = pl.BlockSpec((w_buf * N_KV, HEAD_DIM), lambda bb, jt, ix, pt, sl: (bb, 0))
    o_spec = pl.BlockSpec((None, 1, ATT_WIDTH), lambda bb, jt, ix, pt, sl: (bb, 0, 0))
    n_q = N_KV * SUBLANE
    return pl.pallas_call(
        functools.partial(_sattn2_kernel, past_len=past_len, w_buf=w_buf),
        grid_spec=pltpu.PrefetchScalarGridSpec(
            num_scalar_prefetch=3,
            grid=(b, N_SEL // SEL_PER_STEP),
            in_specs=[row(C_Q, ATT_WIDTH), row(C_KS, KV_WIDTH), row(C_VS, KV_WIDTH), row(C_KW, KV_WIDTH),
                      row(C_VW, KV_WIDTH)] + blk_specs + blk_specs + [win_spec, win_spec],
            out_specs=[o_spec, o_spec, win_spec, win_spec],
            scratch_shapes=[
                pltpu.VMEM((n_q, HEAD_DIM), F32),
                pltpu.VMEM((n_q, 1), F32),
                pltpu.VMEM((n_q, 1), F32),
                pltpu.VMEM((n_q, HEAD_DIM), F32),
            ],
        ),
        out_shape=[jax.ShapeDtypeStruct((b, 1, ATT_WIDTH), F32)] * 2
        + [jax.ShapeDtypeStruct(win_k.shape, F32)] * 2,
        compiler_params=_cparams(("parallel", "arbitrary")),
        name="sample_attn_sel_win",
    )(idx, page_table, slopes, *([sproj3] * 5), *([pool_k] * (N_KV * SEL_PER_STEP)),
      *([pool_v] * (N_KV * SEL_PER_STEP)), win_k, win_v)


def _merge_kernel(oc_ref, os_ref, ow_ref, gl_ref, za_ref, mgc_ref, mga_ref, cd_ref, x_ref, wa_ref, wo_ref, gp_ref,
                  y_ref):
    gates = _sigmoid(gl_ref[...])
    tm = gates.shape[0]
    pieces = []
    for h in range(N_HEADS):
        cols = slice(h * HEAD_DIM, (h + 1) * HEAD_DIM)
        gate = lambda c: jnp.broadcast_to(gates[:, c * N_HEADS + h:c * N_HEADS + h + 1], (tm, HEAD_DIM))
        pieces.append(gate(0) * oc_ref[:, cols] + gate(1) * os_ref[:, cols] + gate(2) * ow_ref[:, cols])
    o = jnp.concatenate(pieces, axis=1)
    attn_d = jnp.dot((o * _silu(za_ref[...])).astype(BF16), wa_ref[...], preferred_element_type=F32)
    mix = _sigmoid(mgc_ref[...]) * cd_ref[...] + _sigmoid(mga_ref[...]) * attn_d
    z = jnp.dot(mix.astype(BF16), wo_ref[...], preferred_element_type=F32)
    ms = jnp.mean(z * z, axis=-1, keepdims=True)
    y_ref[...] = x_ref[...] + z * lax.rsqrt(ms + RMS_EPS) * gp_ref[...]


def _merge(o_c, o_s, o_w, proj, conv_d, x2d, w_ao, w_o, g_post):
    m = x2d.shape[0]
    tm = min(128, m)
    wide = lambda col: pl.BlockSpec((tm, D_MODEL), lambda i: (i, col // D_MODEL))
    plain = pl.BlockSpec((tm, D_MODEL), lambda i: (i, 0))
    return pl.pallas_call(
        _merge_kernel,
        grid=(m // tm,),
        in_specs=[plain, plain, plain,
                  pl.BlockSpec((tm, LANE), lambda i: (i, C_GL // LANE)),
                  wide(C_ZA), wide(C_MGC), wide(C_MGA), plain, plain,
                  _const_spec((ATT_WIDTH, D_MODEL)), _const_spec((D_MODEL, D_MODEL)), _const_spec((1, D_MODEL))],
        out_specs=plain,
        out_shape=jax.ShapeDtypeStruct((m, D_MODEL), F32),
        compiler_params=_cparams(("parallel",)),
        name="merge",
    )(o_c, o_s, o_w, proj, proj, proj, proj, conv_d, x2d, w_ao, w_o, g_post.reshape(1, D_MODEL))


def _reorder_w_in(w_in):
    idx = [int(v) for v in np.cumsum(IN_SIZES)[:-1]]
    (u, ug, zc, q, kc, vc, ks, vs, kw, vw, za, gl, mgc, mga) = jnp.split(w_in, idx, axis=1)
    pad = jnp.zeros((D_MODEL, PROJ_W - (C_GL + 3 * N_HEADS)), w_in.dtype)
    return jnp.concatenate([u, ug, zc, q, za, mgc, mga, kc, vc, ks, vs, kw, vw, gl, pad], axis=1).astype(BF16)


def _alibi_slopes():
    h = jnp.arange(1, N_HEADS + 1, dtype=F32)
    return jnp.exp2(-8.0 * h / N_HEADS)


def _layer_weights(l, w_in, w_dw, w_conv_out, w_attn_out, w_out):
    w_dw_p = jnp.concatenate([w_dw[l], jnp.zeros((HALO - CONV_K, CONV_C), F32)], axis=0)
    return (_reorder_w_in(w_in[l]), w_dw_p, w_conv_out[l].astype(BF16), w_attn_out[l].astype(BF16),
            w_out[l].astype(BF16))


def _prompt_layer(x, slopes, w_r, w_dw_p, w_co, w_ao, w_o, g_pre, b_dw, ln_g, ln_b, pe_k, w1_k, w2_k, pe_v, w1_v,
                  w2_v, g_post):
    t = x.shape[1]
    x2d = x.reshape(t, D_MODEL)
    proj = _project(x2d, g_pre, w_r)
    conv_d, tail = _conv_branch(proj, w_dw_p, b_dw, ln_g, ln_b, w_co)
    kv = [proj[:, c:c + KV_WIDTH] for c in (C_KC, C_VC, C_KS, C_VS, C_KW, C_VW)]
    n_pages = t // PAGE_SIZE
    pt = jnp.arange(n_pages, dtype=jnp.int32).reshape(1, n_pages)
    view = lambda a: a.reshape(t * N_KV, HEAD_DIM)
    kcs, vcs = _compress(pt, view(kv[0]), view(kv[1]), pe_k, w1_k, w2_k, pe_v, w1_v, w2_v, pps=min(16, n_pages))
    o_c, member = _prompt_attn1(slopes, proj, kcs[0], vcs[0])
    o_s = _prompt_flash(slopes, proj, member, _sel_expand_matrix_t(t), window=False)
    o_w = _prompt_flash(slopes, proj, None, None, window=True)
    y = _merge(o_c, o_s, o_w, proj, conv_d, x2d, w_ao, w_o, g_post)
    wl = min(WINDOW, t)
    heads = lambda a: a.reshape(1, -1, N_KV, HEAD_DIM)
    states = (heads(kv[0]), heads(kv[1]), heads(kv[2]), heads(kv[3]), heads(kv[4][-wl:]), heads(kv[5][-wl:]),
              tail[-(CONV_K - 1):].reshape(1, CONV_K - 1, CONV_C))
    return y.reshape(1, t, D_MODEL), states


def _sample_layer(x, l, ck_cmp, cv_cmp, ck_slc, cv_slc, ck_win, cv_win, c_conv, page_table, slopes, w_r, w_dw_p,
                  w_co, w_ao, w_o, g_pre, b_dw, ln_g, ln_b, pe_k, w1_k, w2_k, pe_v, w1_v, w2_v, g_post):
    b = x.shape[0]
    n_pool = ck_cmp.shape[1]
    n_pages = page_table.shape[1]
    past_len = n_pages * PAGE_SIZE
    w_buf = ck_win.shape[2]
    x2d = x.reshape(b, D_MODEL)
    sproj = _project(x2d, g_pre, w_r)
    sproj3 = sproj.reshape(b, 1, PROJ_W)
    conv_d, s_conv_t = _sample_conv(jnp.transpose(c_conv[l], (1, 0, 2)), sproj, w_dw_p, b_dw, ln_g, ln_b, w_co)
    s_conv = jnp.transpose(s_conv_t, (1, 0, 2))
    pool = lambda a: a[l].reshape(n_pool * PAGE_ROWS, HEAD_DIM)
    win = lambda a: a[l].reshape(b * w_buf * N_KV, HEAD_DIM)
    kcs, vcs = _compress(page_table, pool(ck_cmp), pool(cv_cmp), pe_k, w1_k, w2_k, pe_v, w1_v, w2_v,
                         pps=min(16, n_pages))
    o_c, idx = _sample_attn1(slopes, sproj3, kcs, vcs, past_len)
    o_s, o_w, kwo, vwo = _sample_attn2(idx[:, :N_KV, :N_SEL], page_table, slopes, sproj3, pool(ck_slc), pool(cv_slc),
                                       win(ck_win), win(cv_win), past_len)
    y = _merge(o_c.reshape(b, ATT_WIDTH), o_s.reshape(b, ATT_WIDTH), o_w.reshape(b, ATT_WIDTH), sproj, conv_d, x2d,
               w_ao, w_o, g_post)
    heads = lambda a: a.reshape(b, -1, N_KV, HEAD_DIM)
    new = lambda c: sproj[:, c:c + KV_WIDTH].reshape(b, 1, N_KV, HEAD_DIM)
    states = (new(C_KC), new(C_VC), new(C_KS), new(C_VS), heads(kwo), heads(vwo), s_conv)
    return y.reshape(b, 1, D_MODEL), states


def kernel(x_prompt, x_sample, cache_k_cmp, cache_v_cmp, cache_k_slc, cache_v_slc, cache_k_win, cache_v_win,
           cache_conv, page_table, g_pre, w_in, w_dw, b_dw, ln_g, ln_b, w_conv_out, pe_k, w1_k, w2_k, pe_v, w1_v,
           w2_v, w_attn_out, w_out, g_post):
    depth = g_pre.shape[0]
    slopes = _alibi_slopes()
    y_prompt, y_sample = x_prompt, x_sample
    p_states, s_states = [], []
    for l in range(depth):
        mats = _layer_weights(l, w_in, w_dw, w_conv_out, w_attn_out, w_out)
        vecs = (g_pre[l], b_dw[l], ln_g[l], ln_b[l], pe_k[l], w1_k[l], w2_k[l], pe_v[l], w1_v[l], w2_v[l], g_post[l])
        y_prompt, sp = _prompt_layer(y_prompt, slopes, *mats, *vecs)
        y_sample, ss = _sample_layer(y_sample, l, cache_k_cmp, cache_v_cmp, cache_k_slc, cache_v_slc, cache_k_win,
                                     cache_v_win, cache_conv, page_table, slopes, *mats, *vecs)
        p_states.append(sp)
        s_states.append(ss)
    p_out = [jnp.stack(a) for a in zip(*p_states)]
    s_out = [jnp.stack(a) for a in zip(*s_states)]
    return (y_prompt, y_sample, *p_out, *s_out)
```

```python
import functools

import numpy as np
import jax
import jax.numpy as jnp
from jax import lax
from jax.experimental import pallas as pl
from jax.experimental.pallas import tpu as pltpu

F32 = jnp.float32
BF16 = jnp.bfloat16

D_MODEL = 2048
CONV_C = D_MODEL
CONV_K = 31
N_HEADS = 16
N_KV = 4
GROUP = N_HEADS // N_KV
HEAD_DIM = 128
ATT_WIDTH = N_HEADS * HEAD_DIM
KV_WIDTH = N_KV * HEAD_DIM
CMP_LEN = 32
CMP_STRIDE = 16
SEL_BLOCK = 64
N_SEL = 16
N_LOCAL = 2
WINDOW = 512
RMS_EPS = 1e-6
LN_EPS = 1e-5
FORCE_BONUS = 1e9
PAGE_SIZE = 128
IN_SIZES = (CONV_C, CONV_C, CONV_C, ATT_WIDTH) + (KV_WIDTH,) * 6 + (ATT_WIDTH, 3 * N_HEADS, D_MODEL, D_MODEL)

LANE = 128
SUBLANE = 8
NEG_INF = float("-inf")
SCALE = HEAD_DIM ** -0.5

C_U, C_UG, C_ZC, C_Q, C_ZA, C_MGC, C_MGA = (i * D_MODEL for i in range(7))
C_KC = 7 * D_MODEL
C_VC, C_KS, C_VS, C_KW, C_VW = (C_KC + i * KV_WIDTH for i in range(1, 6))
C_GL = C_KC + 6 * KV_WIDTH
PROJ_TN = 1280
PROJ_W = 14 * PROJ_TN
VMEM_LIMIT = 56 * 1024 * 1024


def _cparams(sem):
    return pltpu.CompilerParams(dimension_semantics=sem, vmem_limit_bytes=VMEM_LIMIT)


def _const_spec(shape):
    nd = len(shape)
    return pl.BlockSpec(shape, lambda *a: (0,) * nd, pipeline_mode=pl.Buffered(1))


def _sigmoid(x):
    return jax.nn.sigmoid(x)


def _silu(x):
    return x * jax.nn.sigmoid(x)


def _dot_t(a, b):
    return lax.dot_general(a, b, (((1,), (1,)), ((), ())), preferred_element_type=F32)


def _stack_rows(rows, n):
    w = rows[0].shape[1]
    rid = lax.broadcasted_iota(jnp.int32, (n, 1), 0)
    out = jnp.zeros((n, w), rows[0].dtype)
    for r, v in enumerate(rows):
        out = jnp.where(rid == r, jnp.broadcast_to(v, (n, w)), out)
    return out


def _take_row(x, r):
    rid = lax.broadcasted_iota(jnp.int32, (x.shape[0], 1), 0)
    return jnp.sum(jnp.where(rid == r, x, 0.0), axis=0, keepdims=True)


def _proj_kernel(x_ref, g_ref, w_ref, o_ref, h_ref):
    @pl.when(pl.program_id(1) == 0)
    def _():
        x = x_ref[...]
        ms = jnp.mean(x * x, axis=-1, keepdims=True)
        h_ref[...] = (x * lax.rsqrt(ms + RMS_EPS) * g_ref[...]).astype(BF16)

    o_ref[...] = jnp.dot(h_ref[...], w_ref[...], preferred_element_type=F32)


def _project(x2d, g_pre, w_r):
    m = x2d.shape[0]
    tm = min(512, m)
    return pl.pallas_call(
        _proj_kernel,
        grid=(m // tm, PROJ_W // PROJ_TN),
        in_specs=[
            pl.BlockSpec((tm, D_MODEL), lambda i, j: (i, 0)),
            pl.BlockSpec((1, D_MODEL), lambda i, j: (0, 0)),
            pl.BlockSpec((D_MODEL, PROJ_TN), lambda i, j: (0, j)),
        ],
        out_specs=pl.BlockSpec((tm, PROJ_TN), lambda i, j: (i, j)),
        out_shape=jax.ShapeDtypeStruct((m, PROJ_W), F32),
        scratch_shapes=[pltpu.VMEM((tm, D_MODEL), BF16)],
        compiler_params=_cparams(("parallel", "arbitrary")),
        name="proj",
    )(x2d, g_pre.reshape(1, D_MODEL), w_r)


KV_COLS = (C_KC, C_VC, C_KS, C_VS, C_KW, C_VW)


def _kv_rows_kernel(*refs):
    n = len(refs) // 2
    for src, dst in zip(refs[:n], refs[n:]):
        tm = src.shape[0]
        for g in range(N_KV):
            dst[pl.ds(g, tm, stride=N_KV), :] = src[:, g * HEAD_DIM:(g + 1) * HEAD_DIM]


def _kv_rows(proj):
    m = proj.shape[0]
    tm = min(512, m)
    col_spec = lambda c: pl.BlockSpec((tm, KV_WIDTH), lambda i: (i, c // KV_WIDTH))
    return pl.pallas_call(
        _kv_rows_kernel,
        grid=(m // tm,),
        in_specs=[col_spec(c) for c in KV_COLS],
        out_specs=[pl.BlockSpec((tm * N_KV, HEAD_DIM), lambda i: (i, 0))] * len(KV_COLS),
        out_shape=[jax.ShapeDtypeStruct((m * N_KV, HEAD_DIM), F32)] * len(KV_COLS),
        compiler_params=_cparams(("parallel",)),
        name="kv_rows",
    )(*([proj] * len(KV_COLS)))


HALO = 32
CONV_RB = 8


def _ln_gate_out(c, zc, bdw, lng, lnb, wco_ref):
    c = c + bdw
    mean = jnp.mean(c, axis=-1, keepdims=True)
    xc = c - mean
    var = jnp.mean(xc * xc, axis=-1, keepdims=True)
    y = xc * lax.rsqrt(var + LN_EPS) * lng + lnb
    act = _silu(y) * _silu(zc)
    return jnp.dot(act.astype(BF16), wco_ref[...], preferred_element_type=F32)


def _conv_kernel(u_ref, ug_ref, zc_ref, uh_ref, ugh_ref, wdw_ref, bdw_ref, lng_ref, lnb_ref, wco_ref,
                 out_ref, tail_ref, uext_ref, sh_ref, c_ref):
    i = pl.program_id(0)
    tm = u_ref.shape[0]
    glu = u_ref[...] * _sigmoid(ug_ref[...])
    halo = uh_ref[...] * _sigmoid(ugh_ref[...])
    not_first = (jnp.zeros((HALO, 1), jnp.int32) + i) > 0
    uext_ref[0:HALO, :] = jnp.where(not_first, halo, 0.0)
    uext_ref[HALO:HALO + tm, :] = glu
    tail_ref[...] = glu[tm - HALO:, :]
    n_sh = tm + HALO - SUBLANE
    for p in range(1, SUBLANE):
        sh_ref[p - 1] = uext_ref[p:p + n_sh, :]

    off = HALO - (CONV_K - 1)

    def body(rc, carry):
        r0 = pl.multiple_of(rc * CONV_RB, CONV_RB)
        acc = jnp.zeros((CONV_RB, CONV_C), F32)
        for k in range(CONV_K):
            p, base = (off + k) % SUBLANE, (off + k) // SUBLANE * SUBLANE
            if p == 0:
                win = uext_ref[pl.ds(r0 + base, CONV_RB), :]
            else:
                win = sh_ref[p - 1, pl.ds(r0 + base, CONV_RB), :]
            acc = acc + win * wdw_ref[k]
        c_ref[pl.ds(r0, CONV_RB), :] = acc
        return carry

    lax.fori_loop(0, tm // CONV_RB, body, 0)
    out_ref[...] = _ln_gate_out(c_ref[...], zc_ref[...], bdw_ref[...], lng_ref[...], lnb_ref[...], wco_ref)


def _conv_branch(proj, w_dw_p, b_dw, ln_g, ln_b, w_co):
    t = proj.shape[0]
    tm = min(256, t)
    hb = tm // HALO
    row = lambda a: a.reshape(1, CONV_C)
    return pl.pallas_call(
        _conv_kernel,
        grid=(t // tm,),
        in_specs=[
            pl.BlockSpec((tm, CONV_C), lambda i: (i, C_U // CONV_C)),
            pl.BlockSpec((tm, CONV_C), lambda i: (i, C_UG // CONV_C)),
            pl.BlockSpec((tm, CONV_C), lambda i: (i, C_ZC // CONV_C)),
            pl.BlockSpec((HALO, CONV_C), lambda i: (jnp.maximum(i * hb - 1, 0), C_U // CONV_C)),
            pl.BlockSpec((HALO, CONV_C), lambda i: (jnp.maximum(i * hb - 1, 0), C_UG // CONV_C)),
            _const_spec((CONV_K, CONV_RB, CONV_C)),
            _const_spec((1, CONV_C)),
            _const_spec((1, CONV_C)),
            _const_spec((1, CONV_C)),
            _const_spec((CONV_C, D_MODEL)),
        ],
        out_specs=[
            pl.BlockSpec((tm, D_MODEL), lambda i: (i, 0)),
            pl.BlockSpec((HALO, CONV_C), lambda i: (0, 0)),
        ],
        out_shape=[jax.ShapeDtypeStruct((t, D_MODEL), F32), jax.ShapeDtypeStruct((HALO, CONV_C), F32)],
        scratch_shapes=[pltpu.VMEM((tm + HALO, CONV_C), F32),
                        pltpu.VMEM((SUBLANE - 1, tm + HALO - SUBLANE, CONV_C), F32),
                        pltpu.VMEM((tm, CONV_C), F32)],
        compiler_params=_cparams(("arbitrary",)),
        name="conv_branch",
    )(proj, proj, proj, proj, proj,
      jnp.broadcast_to(w_dw_p[:CONV_K, None, :], (CONV_K, CONV_RB, CONV_C)),
      row(b_dw), row(ln_g), row(ln_b), w_co)


def _sconv_kernel(cc_ref, u_ref, ug_ref, zc_ref, wdw_ref, bdw_ref, lng_ref, lnb_ref, wco_ref,
                  out_ref, state_ref):
    nh = CONV_K - 1
    glu = u_ref[...] * _sigmoid(ug_ref[...])
    c = glu * wdw_ref[nh:nh + 1, :]
    for k in range(nh):
        c = c + cc_ref[k] * wdw_ref[k:k + 1, :]
    out_ref[...] = _ln_gate_out(c, zc_ref[...], bdw_ref[...], lng_ref[...], lnb_ref[...], wco_ref)
    for k in range(nh - 1):
        state_ref[k] = cc_ref[k + 1]
    state_ref[nh - 1] = glu


def _sample_conv(cc, sproj, w_dw_p, b_dw, ln_g, ln_b, w_co):
    nh, b = cc.shape[0], cc.shape[1]
    row = lambda a: a.reshape(1, CONV_C)
    return pl.pallas_call(
        _sconv_kernel,
        grid=(1,),
        in_specs=[
            pl.BlockSpec((nh, b, CONV_C), lambda i: (0, 0, 0)),
            pl.BlockSpec((b, CONV_C), lambda i: (0, C_U // CONV_C)),
            pl.BlockSpec((b, CONV_C), lambda i: (0, C_UG // CONV_C)),
            pl.BlockSpec((b, CONV_C), lambda i: (0, C_ZC // CONV_C)),
            pl.BlockSpec((HALO, CONV_C), lambda i: (0, 0)),
            pl.BlockSpec((1, CONV_C), lambda i: (0, 0)),
            pl.BlockSpec((1, CONV_C), lambda i: (0, 0)),
            pl.BlockSpec((1, CONV_C), lambda i: (0, 0)),
            pl.BlockSpec((CONV_C, D_MODEL), lambda i: (0, 0)),
        ],
        out_specs=[
            pl.BlockSpec((b, D_MODEL), lambda i: (0, 0)),
            pl.BlockSpec((nh, b, CONV_C), lambda i: (0, 0, 0)),
        ],
        out_shape=[jax.ShapeDtypeStruct((b, D_MODEL), F32), jax.ShapeDtypeStruct((nh, b, CONV_C), F32)],
        compiler_params=_cparams(("arbitrary",)),
        name="sample_conv",
    )(cc, sproj, sproj, sproj, w_dw_p, row(b_dw), row(ln_g), row(ln_b), w_co)


SUB_PER_PAGE = PAGE_SIZE // CMP_STRIDE
PAGE_ROWS = PAGE_SIZE * N_KV
R_N = CMP_LEN // CMP_STRIDE
STEP_PERM = tuple(range(0, CMP_STRIDE, 2)) + tuple(range(1, CMP_STRIDE, 2))
HALF_SUB = SUBLANE // 2


def _compress_kernel(pt_ref, *refs, pps):
    del pt_ref
    pages = (refs[:pps], refs[pps:2 * pps])
    w01s = refs[2 * pps:2 * pps + 2]
    pes = refs[2 * pps + 2:2 * pps + 4]
    w2s = refs[2 * pps + 4:2 * pps + 6]
    outs = refs[2 * pps + 6:2 * pps + 8]
    carries = refs[2 * pps + 8:2 * pps + 10]
    xs_refs = refs[2 * pps + 10:2 * pps + 12]
    rows = pps * SUB_PER_PAGE * N_KV
    n_c = CMP_STRIDE // 2
    grp = n_c * SUBLANE

    @pl.when(pl.program_id(1) == 0)
    def _():
        carries[0][...] = jnp.zeros_like(carries[0])
        carries[1][...] = jnp.zeros_like(carries[1])

    top3 = lax.broadcasted_iota(jnp.int32, (n_c, SUBLANE, HEAD_DIM), 1) < HALF_SUB
    top = lax.broadcasted_iota(jnp.int32, (SUBLANE, HEAD_DIM), 0) < HALF_SUB
    for which in range(2):
        xs_ref = xs_refs[which]
        for i, pg in enumerate(pages[which]):
            los, his = [], []
            for p in range(SUB_PER_PAGE // 2):
                a = pg[(2 * p) * grp:(2 * p + 1) * grp, :].reshape(n_c, SUBLANE, HEAD_DIM)
                b = pg[(2 * p + 1) * grp:(2 * p + 2) * grp, :].reshape(n_c, SUBLANE, HEAD_DIM)
                los.append(jnp.where(top3, a, pltpu.roll(b, HALF_SUB, 1)))
                his.append(jnp.where(top3, pltpu.roll(a, HALF_SUB, 1), b))
            for pp in range(len(los) // 2):
                r0 = (i * (SUB_PER_PAGE // 2) + 2 * pp) * SUBLANE
                for c in range(n_c):
                    for half, parts in ((0, los), (1, his)):
                        col = (half * n_c + c) * HEAD_DIM
                        xs_ref[r0:r0 + 2 * SUBLANE, col:col + HEAD_DIM] = jnp.concatenate(
                            [parts[2 * pp][c], parts[2 * pp + 1][c]], axis=0).astype(BF16)
        w01 = w01s[which][...]
        a = jnp.dot(xs_ref[...], w01, preferred_element_type=F32)
        pe_o = jnp.dot(pes[which][...].astype(BF16), w01, preferred_element_type=F32)
        hid0 = pe_o[0:1, :HEAD_DIM] + pe_o[1:2, HEAD_DIM:]
        carry = carries[which]
        a0, a1 = a[:, :HEAD_DIM], a[:, HEAD_DIM:]
        a0s = pltpu.roll(a0, N_KV, 0)
        first = jnp.where(top, carry[...], a0s[0:SUBLANE])
        a0s = jnp.concatenate([first, a0s[SUBLANE:]], axis=0)
        carry[...] = pltpu.roll(a0[rows - SUBLANE:rows, :], N_KV, 0)
        hid = hid0 + a0s + a1
        outs[which][...] = jnp.dot(_silu(hid).astype(BF16), w2s[which][...], preferred_element_type=F32)


def _w01(w1):
    w = w1.reshape(R_N, CMP_STRIDE, HEAD_DIM, HEAD_DIM)[:, STEP_PERM, :, :]
    w = w.reshape(R_N, CMP_STRIDE * HEAD_DIM, HEAD_DIM)
    return jnp.concatenate([w[r] for r in range(R_N)], axis=1).astype(BF16)


def _pe2(pe):
    p = pe.reshape(R_N, CMP_STRIDE, HEAD_DIM)[:, STEP_PERM, :].reshape(R_N, CMP_STRIDE * HEAD_DIM)
    return jnp.concatenate([p, jnp.zeros((SUBLANE - R_N, CMP_STRIDE * HEAD_DIM), F32)], axis=0)


def _compress(page_table, cache_k, cache_v, pe_k, w1_k, w2_k, pe_v, w1_v, w2_v, pps):
    b, n_pages = page_table.shape
    rows = pps * SUB_PER_PAGE * N_KV
    n_sub = n_pages * SUB_PER_PAGE

    def page_spec(i):
        return pl.BlockSpec((PAGE_ROWS, HEAD_DIM), lambda bb, c, pt: (pt[bb, c * pps + i], 0))

    cspec = lambda shape: pl.BlockSpec(shape, lambda bb, c, pt: (0,) * len(shape))
    in_specs = [page_spec(i) for i in range(pps)] * 2 + [
        cspec((CMP_STRIDE * HEAD_DIM, R_N * HEAD_DIM))] * 2 + [
        cspec((SUBLANE, CMP_STRIDE * HEAD_DIM))] * 2 + [cspec((HEAD_DIM, HEAD_DIM))] * 2
    out_spec = pl.BlockSpec((None, rows, HEAD_DIM), lambda bb, c, pt: (bb, c, 0))
    return pl.pallas_call(
        functools.partial(_compress_kernel, pps=pps),
        grid_spec=pltpu.PrefetchScalarGridSpec(
            num_scalar_prefetch=1,
            grid=(b, n_pages // pps),
            in_specs=in_specs,
            out_specs=[out_spec, out_spec],
            scratch_shapes=[pltpu.VMEM((SUBLANE, HEAD_DIM), F32)] * 2
            + [pltpu.VMEM((rows, CMP_STRIDE * HEAD_DIM), BF16)] * 2,
        ),
        out_shape=[jax.ShapeDtypeStruct((b, n_sub * N_KV, HEAD_DIM), F32)] * 2,
        compiler_params=_cparams(("parallel", "arbitrary")),
        name="compress",
    )(page_table, *([cache_k] * pps), *([cache_v] * pps), _w01(w1_k), _w01(w1_v), _pe2(pe_k), _pe2(pe_v),
      w2_k.astype(BF16), w2_v.astype(BF16))


def _ovl_shifted(n_sub, n_lanes):
    i = (np.arange(n_sub)[:, None] - 1) * CMP_STRIDE
    j = np.arange(n_lanes)[None, :] * SEL_BLOCK
    ov = np.clip(np.minimum(i + CMP_LEN, j + SEL_BLOCK) - np.maximum(i, j), 0, None).astype(np.float32) / CMP_LEN
    ov[0, :] = 0.0
    return jnp.asarray(ov, dtype=BF16)


def _softmax_rows(s, valid):
    s = jnp.where(valid, s, NEG_INF)
    mx = jnp.max(s, axis=-1, keepdims=True)
    mx = jnp.where(mx == NEG_INF, 0.0, mx)
    p = jnp.exp(s - mx)
    den = jnp.maximum(jnp.sum(p, axis=-1, keepdims=True), 1e-30)
    return p * (1.0 / den)


def _select_blocks(imp, qpos, n_sel_blocks):
    r, l = imp.shape
    lane = lax.broadcasted_iota(jnp.int32, (r, l), 1)
    lanef = lane.astype(F32)
    cur = lax.shift_right_logical(qpos, int(np.log2(SEL_BLOCK)))
    valid = lane * SEL_BLOCK <= qpos
    forced = (lane == 0) | ((lane <= cur) & (lane > cur - N_LOCAL))
    score = jnp.where(valid, imp + jnp.where(forced, FORCE_BONUS, 0.0), -FORCE_BONUS)
    score = jnp.where(lane < n_sel_blocks, score, -3e38)
    member = jnp.zeros((r, l), F32)
    picks = []
    for _ in range(N_SEL):
        mx = jnp.max(score, axis=-1, keepdims=True)
        idx = jnp.min(jnp.where(score == mx, lanef, float(l)), axis=-1, keepdims=True)
        hit = lanef == idx
        member = jnp.where(hit, 1.0, member)
        score = jnp.where(hit, NEG_INF, score)
        picks.append(idx)
    return picks, member


def _attn1_kernel(sl_ref, q_ref, kc_ref, vc_ref, ovl_ref, tile_ref, oc_ref, mem_ref, act_ref, *, tq, n_sub,
                  n_sel_blocks):
    qi = pl.program_id(0)
    qpos = qi * tq + lax.broadcasted_iota(jnp.int32, (tq, 1), 0)
    m = lax.broadcasted_iota(jnp.int32, (1, n_sub), 1)
    c_end = (m - 1) * CMP_STRIDE + (CMP_LEN - 1)
    c_ctr = ((m - 1) * CMP_STRIDE).astype(F32) + (CMP_LEN - 1) / 2
    valid = (m >= 1) & (c_end <= qpos)
    dist = jnp.abs(qpos.astype(F32) - c_ctr)
    any_blk = []
    for g in range(N_KV):
        qg = jnp.concatenate([q_ref[:, (g * GROUP + r) * HEAD_DIM:(g * GROUP + r + 1) * HEAD_DIM]
                              for r in range(GROUP)], axis=0).astype(BF16)
        kg = kc_ref[pl.ds(g, n_sub, stride=N_KV), :].astype(BF16)
        s = _dot_t(qg, kg)
        ps = []
        for r in range(GROUP):
            sr = s[r * tq:(r + 1) * tq] * SCALE - sl_ref[g * GROUP + r] * dist
            ps.append(_softmax_rows(sr, valid))
        psum = ps[0]
        for r in range(1, GROUP):
            psum = psum + ps[r]
        pcat = jnp.concatenate(ps, axis=0).astype(BF16)
        oc = jnp.dot(pcat, vc_ref[pl.ds(g, n_sub, stride=N_KV), :].astype(BF16), preferred_element_type=F32)
        for r in range(GROUP):
            oc_ref[:, (g * GROUP + r) * HEAD_DIM:(g * GROUP + r + 1) * HEAD_DIM] = oc[r * tq:(r + 1) * tq]
        imp = jnp.dot(psum.astype(BF16), ovl_ref[...], preferred_element_type=F32)
        _, member = _select_blocks(imp, qpos, n_sel_blocks)
        mem_ref[g] = member.astype(BF16)
        any_blk.append(jnp.max(member, axis=0, keepdims=True))
    hits = jnp.dot(_stack_rows(any_blk, SUBLANE).astype(BF16), tile_ref[...], preferred_element_type=F32)
    act_ref[...] = (hits > 0.5).astype(jnp.int32)


def _prompt_attn1(slopes, proj, kcs, vcs):
    t = proj.shape[0]
    n_sub = kcs.shape[0] // N_KV
    tq = min(ATTN1_TQ, t)
    n_sel_blocks = -(-t // SEL_BLOCK)
    blocks_per_tile = min(SEL_TK, t) // SEL_BLOCK
    assert n_sel_blocks <= LANE and n_sel_blocks <= blocks_per_tile * N_KT_LANES
    ovl = _ovl_shifted(n_sub, LANE)
    tile_of_block = jnp.asarray(np.arange(LANE)[:, None] // blocks_per_tile == np.arange(LANE)[None, :], dtype=BF16)
    o_c, member, act = pl.pallas_call(
        functools.partial(_attn1_kernel, tq=tq, n_sub=n_sub, n_sel_blocks=n_sel_blocks),
        grid_spec=pltpu.PrefetchScalarGridSpec(
            num_scalar_prefetch=1,
            grid=(t // tq,),
            in_specs=[
                pl.BlockSpec((tq, ATT_WIDTH), lambda i, sl: (i, C_Q // ATT_WIDTH)),
                pl.BlockSpec((n_sub * N_KV, HEAD_DIM), lambda i, sl: (0, 0)),
                pl.BlockSpec((n_sub * N_KV, HEAD_DIM), lambda i, sl: (0, 0)),
                pl.BlockSpec((n_sub, LANE), lambda i, sl: (0, 0)),
                pl.BlockSpec((LANE, LANE), lambda i, sl: (0, 0)),
            ],
            out_specs=[
                pl.BlockSpec((tq, ATT_WIDTH), lambda i, sl: (i, 0)),
                pl.BlockSpec((N_KV, tq, LANE), lambda i, sl: (0, i, 0)),
                pl.BlockSpec((SUBLANE, LANE), lambda i, sl: (i, 0)),
            ],
        ),
        out_shape=[jax.ShapeDtypeStruct((t, ATT_WIDTH), F32), jax.ShapeDtypeStruct((N_KV, t, LANE), BF16),
                   jax.ShapeDtypeStruct((t // tq * SUBLANE, LANE), jnp.int32)],
        compiler_params=_cparams(("parallel",)),
        name="prompt_attn_cmp",
    )(slopes, proj, kcs, vcs, ovl, tile_of_block)
    return o_c, member, act[:, :N_KT_LANES].reshape(-1)


LOG2E = float(np.log2(np.e))


SEL_TQ = 256
SEL_TK = 1024
ATTN1_TQ = 256
N_KT_LANES = LANE // (SEL_TK // SEL_BLOCK)


def _stack_heads(q_ref, tq):
    return jnp.concatenate([(q_ref[:, r * HEAD_DIM:(r + 1) * HEAD_DIM] * (SCALE * LOG2E)).astype(BF16)
                            for r in range(GROUP)], axis=0)


def _sel_kernel(sl_ref, qt_ref, kt_ref, fl_ref, act_ref, q_ref, k_ref, v_ref, mem_ref, et_ref, o_ref,
                qs_ref, m_ref, l_ref, acc_ref, *, tq, tk):
    g = pl.program_id(0)
    step = pl.program_id(1)
    qi = qt_ref[step]
    kt = kt_ref[step]
    flags = fl_ref[step]

    @pl.when((flags & 1) != 0)
    def _():
        qs_ref[...] = _stack_heads(q_ref, tq)
        m_ref[...] = jnp.full(m_ref.shape, NEG_INF, F32)
        l_ref[...] = jnp.zeros(l_ref.shape, F32)
        acc_ref[...] = jnp.zeros(acc_ref.shape, F32)

    sub = tq // ATTN1_TQ
    active = act_ref[((qi * sub) * SUBLANE + g) * N_KT_LANES + kt]
    for h in range(1, sub):
        active = active | act_ref[((qi * sub + h) * SUBLANE + g) * N_KT_LANES + kt]

    @pl.when(active != 0)
    def _():
        st = _dot_t(k_ref[...].astype(BF16), qs_ref[...])
        kpos = kt * tk + lax.broadcasted_iota(jnp.int32, (tk, 1), 0)
        qpos = qi * tq + lax.broadcasted_iota(jnp.int32, (1, tq), 1)
        mask = (_dot_t(et_ref[...], mem_ref[...]) > 0.5) & (kpos <= qpos)
        koff = (kpos - qi * tq).astype(F32)
        vt = v_ref[...].T.astype(BF16)
        for r in range(GROUP):
            cols = slice(r * tq, (r + 1) * tq)
            sr = jnp.where(mask, st[:, cols] + (sl_ref[g * GROUP + r] * LOG2E) * koff, NEG_INF)
            m_prev = m_ref[:, cols]
            m_new = jnp.maximum(m_prev, jnp.max(sr, axis=0, keepdims=True))
            m_safe = jnp.where(m_new == NEG_INF, 0.0, m_new)
            p = jnp.exp2(sr - m_safe)
            alpha = jnp.exp2(m_prev - m_safe)
            l_ref[:, cols] = alpha * l_ref[:, cols] + jnp.sum(p, axis=0, keepdims=True)
            acc_ref[:, cols] = alpha * acc_ref[:, cols] + jnp.dot(vt, p.astype(BF16), preferred_element_type=F32)
            m_ref[:, cols] = m_new

    @pl.when((flags & 2) != 0)
    def _():
        for r in range(GROUP):
            cols = slice(r * tq, (r + 1) * tq)
            o = acc_ref[:, cols] * (1.0 / jnp.maximum(l_ref[:, cols], 1e-30))
            o_ref[:, r * HEAD_DIM:(r + 1) * HEAD_DIM] = o.T


def _sel_steps(t, tq, tk):
    qt, kt, fl = [], [], []
    for qi in range(t // tq):
        hi = (qi * tq + tq - 1) // tk
        for k in range(hi + 1):
            qt.append(qi)
            kt.append(k)
            fl.append((1 if k == 0 else 0) | (2 if k == hi else 0))
    as_i32 = lambda a: jnp.asarray(np.asarray(a, np.int32))
    return as_i32(qt), as_i32(kt), as_i32(fl)


def _prompt_sel(slopes, proj, member, active, emat_t):
    t = proj.shape[0]
    tq, tk = min(SEL_TQ, t), min(SEL_TK, t)
    qt, kt, fl = _sel_steps(t, tq, tk)
    imap = lambda f: (lambda g, s, sl, qt, kt, fl, act: f(g, qt[s], kt[s]))
    return pl.pallas_call(
        functools.partial(_sel_kernel, tq=tq, tk=tk),
        grid_spec=pltpu.PrefetchScalarGridSpec(
            num_scalar_prefetch=5,
            grid=(N_KV, int(qt.shape[0])),
            in_specs=[
                pl.BlockSpec((tq, GROUP * HEAD_DIM), imap(lambda g, qi, ki: (qi, C_Q // (GROUP * HEAD_DIM) + g))),
                pl.BlockSpec((tk, HEAD_DIM), imap(lambda g, qi, ki: (ki, C_KS // HEAD_DIM + g))),
                pl.BlockSpec((tk, HEAD_DIM), imap(lambda g, qi, ki: (ki, C_VS // HEAD_DIM + g))),
                pl.BlockSpec((None, tq, LANE), imap(lambda g, qi, ki: (g, qi, 0))),
                pl.BlockSpec((tk, LANE), imap(lambda g, qi, ki: (ki, 0))),
            ],
            out_specs=pl.BlockSpec((tq, GROUP * HEAD_DIM), imap(lambda g, qi, ki: (qi, g))),
            scratch_shapes=[
                pltpu.VMEM((GROUP * tq, HEAD_DIM), BF16),
                pltpu.VMEM((1, GROUP * tq), F32),
                pltpu.VMEM((1, GROUP * tq), F32),
                pltpu.VMEM((HEAD_DIM, GROUP * tq), F32),
            ],
        ),
        out_shape=jax.ShapeDtypeStruct((t, ATT_WIDTH), F32),
        compiler_params=_cparams(("parallel", "arbitrary")),
        name="prompt_attn_sel",
    )(slopes, qt, kt, fl, active, proj, proj, proj, member, emat_t)


def _win_kernel(sl_ref, q_ref, *refs, tq, n_kt):
    k_refs, v_refs, o_ref = refs[:n_kt], refs[n_kt:2 * n_kt], refs[2 * n_kt]
    g = pl.program_id(0)
    qi = pl.program_id(1)
    qs = _stack_heads(q_ref, tq)
    k = jnp.concatenate([kr[...] for kr in k_refs], axis=0).astype(BF16)
    vt = jnp.concatenate([vr[...] for vr in v_refs], axis=0).T.astype(BF16)
    st = _dot_t(k, qs)
    kpos = (qi - (n_kt - 1)) * tq + lax.broadcasted_iota(jnp.int32, (n_kt * tq, 1), 0)
    qpos = qi * tq + lax.broadcasted_iota(jnp.int32, (1, tq), 1)
    rel = kpos - qpos
    mask = (rel <= 0) & (rel > -WINDOW) & (kpos >= 0)
    koff = (kpos - qi * tq).astype(F32)
    for r in range(GROUP):
        sr = jnp.where(mask, st[:, r * tq:(r + 1) * tq] + (sl_ref[g * GROUP + r] * LOG2E) * koff, NEG_INF)
        mx = jnp.max(sr, axis=0, keepdims=True)
        p = jnp.exp2(sr - jnp.where(mx == NEG_INF, 0.0, mx))
        den = jnp.maximum(jnp.sum(p, axis=0, keepdims=True), 1e-30)
        o = jnp.dot(vt, p.astype(BF16), preferred_element_type=F32) * (1.0 / den)
        o_ref[:, r * HEAD_DIM:(r + 1) * HEAD_DIM] = o.T


def _prompt_win(slopes, proj):
    t = proj.shape[0]
    tq = min(SEL_TQ, t)
    n_kt = -(-(WINDOW - 1) // tq) + 1
    kv_spec = lambda col, j: pl.BlockSpec(
        (tq, HEAD_DIM), lambda g, qi, sl: (jnp.maximum(qi - (n_kt - 1) + j, 0), col // HEAD_DIM + g))
    return pl.pallas_call(
        functools.partial(_win_kernel, tq=tq, n_kt=n_kt),
        grid_spec=pltpu.PrefetchScalarGridSpec(
            num_scalar_prefetch=1,
            grid=(N_KV, t // tq),
            in_specs=[pl.BlockSpec((tq, GROUP * HEAD_DIM), lambda g, qi, sl: (qi, C_Q // (GROUP * HEAD_DIM) + g))]
            + [kv_spec(C_KW, j) for j in range(n_kt)] + [kv_spec(C_VW, j) for j in range(n_kt)],
            out_specs=pl.BlockSpec((tq, GROUP * HEAD_DIM), lambda g, qi, sl: (qi, g)),
        ),
        out_shape=jax.ShapeDtypeStruct((t, ATT_WIDTH), F32),
        compiler_params=_cparams(("parallel", "parallel")),
        name="prompt_attn_win",
    )(slopes, *([proj] * (1 + 2 * n_kt)))


def _sel_expand_matrix_t(t):
    e = (np.arange(t)[:, None] // SEL_BLOCK) == np.arange(LANE)[None, :]
    return jnp.asarray(e, dtype=BF16)


def _sattn1_kernel(sl_ref, q_ref, kc_ref, vc_ref, ovl_ref, oc_ref, idx_ref, *, n_sub, qpos, n_sel_blocks, n_lanes):
    m = lax.broadcasted_iota(jnp.int32, (1, n_sub), 1)
    c_end = (m - 1) * CMP_STRIDE + (CMP_LEN - 1)
    c_ctr = ((m - 1) * CMP_STRIDE).astype(F32) + (CMP_LEN - 1) / 2
    valid = (m >= 1) & (c_end <= qpos)
    dist = jnp.abs(float(qpos) - c_ctr)
    qposv = jnp.full((SUBLANE, 1), qpos, jnp.int32)
    out_lane = lax.broadcasted_iota(jnp.int32, (SUBLANE, LANE), 1)
    idx_acc = jnp.zeros((SUBLANE, LANE), F32)
    head_row = lax.broadcasted_iota(jnp.int32, (SUBLANE, 1), 0)
    psums = []
    for g in range(N_KV):
        qg = _stack_rows([q_ref[:, (g * GROUP + r) * HEAD_DIM:(g * GROUP + r + 1) * HEAD_DIM]
                          for r in range(GROUP)], SUBLANE).astype(BF16)
        s = _dot_t(qg, kc_ref[pl.ds(g, n_sub, stride=N_KV), :].astype(BF16))
        slope = jnp.zeros((SUBLANE, 1), F32)
        for r in range(GROUP):
            slope = jnp.where(head_row == r, sl_ref[g * GROUP + r], slope)
        p = _softmax_rows(s * SCALE - slope * dist, valid)
        p = jnp.where(head_row < GROUP, p, 0.0)
        oc = jnp.dot(p.astype(BF16), vc_ref[pl.ds(g, n_sub, stride=N_KV), :].astype(BF16),
                     preferred_element_type=F32)
        for r in range(GROUP):
            oc_ref[:, (g * GROUP + r) * HEAD_DIM:(g * GROUP + r + 1) * HEAD_DIM] = _take_row(oc, r)
        psums.append(jnp.sum(p, axis=0, keepdims=True))
    imp = jnp.dot(_stack_rows(psums, SUBLANE).astype(BF16), ovl_ref[...], preferred_element_type=F32)
    picks, _ = _select_blocks(imp, qposv, n_sel_blocks)
    for j, pk in enumerate(picks):
        idx_acc = jnp.where(out_lane == j, pk, idx_acc)
    idx_ref[...] = idx_acc.astype(jnp.int32)


def _sample_attn1(slopes, sproj3, kcs, vcs, past_len):
    b = sproj3.shape[0]
    n_sub = kcs.shape[1] // N_KV
    n_sel_blocks = -(-(past_len + 1) // SEL_BLOCK)
    n_lanes = -(-n_sel_blocks // LANE) * LANE
    ovl = _ovl_shifted(n_sub, n_lanes)
    return pl.pallas_call(
        functools.partial(_sattn1_kernel, n_sub=n_sub, qpos=past_len, n_sel_blocks=n_sel_blocks, n_lanes=n_lanes),
        grid_spec=pltpu.PrefetchScalarGridSpec(
            num_scalar_prefetch=1,
            grid=(b,),
            in_specs=[
                pl.BlockSpec((None, 1, ATT_WIDTH), lambda i, sl: (i, 0, C_Q // ATT_WIDTH)),
                pl.BlockSpec((None, n_sub * N_KV, HEAD_DIM), lambda i, sl: (i, 0, 0)),
                pl.BlockSpec((None, n_sub * N_KV, HEAD_DIM), lambda i, sl: (i, 0, 0)),
                pl.BlockSpec((n_sub, n_lanes), lambda i, sl: (0, 0)),
            ],
            out_specs=[
                pl.BlockSpec((None, 1, ATT_WIDTH), lambda i, sl: (i, 0, 0)),
                pl.BlockSpec((None, SUBLANE, LANE), lambda i, sl: (i, 0, 0)),
            ],
        ),
        out_shape=[jax.ShapeDtypeStruct((b, 1, ATT_WIDTH), F32), jax.ShapeDtypeStruct((b, SUBLANE, LANE), jnp.int32)],
        compiler_params=_cparams(("parallel",)),
        name="sample_attn_cmp",
    )(slopes, sproj3, kcs, vcs, ovl)


SEL_PER_STEP = 4
BLK_ROWS = SEL_BLOCK * N_KV


def _sattn2_kernel(idx_ref, pt_ref, sl_ref, q_ref, ksn_ref, vsn_ref, kwn_ref, vwn_ref, *refs, past_len, w_buf):
    del pt_ref
    nblk = N_KV * SEL_PER_STEP
    kbs, vbs = refs[:nblk], refs[nblk:2 * nblk]
    kwc_ref, vwc_ref, os_ref, ow_ref, kwo_ref, vwo_ref, qs_ref, m_ref, l_ref, acc_ref = refs[2 * nblk:]
    b = pl.program_id(0)
    jt = pl.program_id(1)
    head_row = lax.broadcasted_iota(jnp.int32, (SUBLANE, 1), 0)

    def slope_col(g):
        slope = jnp.zeros((SUBLANE, 1), F32)
        for r in range(GROUP):
            slope = jnp.where(head_row == r, sl_ref[g * GROUP + r], slope)
        return slope

    @pl.when(jt == 0)
    def _():
        for g in range(N_KV):
            qs_ref[g * SUBLANE:(g + 1) * SUBLANE, :] = _stack_rows(
                [q_ref[:, (g * GROUP + r) * HEAD_DIM:(g * GROUP + r + 1) * HEAD_DIM] for r in range(GROUP)],
                SUBLANE)
        m_ref[...] = jnp.full(m_ref.shape, NEG_INF, F32)
        l_ref[...] = jnp.zeros(l_ref.shape, F32)
        acc_ref[...] = jnp.zeros(acc_ref.shape, F32)

    n_keys = SEL_PER_STEP * SEL_BLOCK
    lane = lax.broadcasted_iota(jnp.int32, (1, n_keys), 1)
    pos_in_blk = lax.broadcasted_iota(jnp.int32, (SEL_BLOCK, 1), 0)
    for g in range(N_KV):
        rows = slice(g * SUBLANE, (g + 1) * SUBLANE)
        kparts, vparts = [], []
        blkv = jnp.zeros((1, n_keys), jnp.int32)
        for jj in range(SEL_PER_STEP):
            blk = idx_ref[b, g, jt * SEL_PER_STEP + jj]
            is_past = (blk * SEL_BLOCK + pos_in_blk) < past_len
            new_k = jnp.broadcast_to(ksn_ref[:, g * HEAD_DIM:(g + 1) * HEAD_DIM], (SEL_BLOCK, HEAD_DIM))
            new_v = jnp.broadcast_to(vsn_ref[:, g * HEAD_DIM:(g + 1) * HEAD_DIM], (SEL_BLOCK, HEAD_DIM))
            kparts.append(jnp.where(is_past, kbs[g * SEL_PER_STEP + jj][pl.ds(g, SEL_BLOCK, stride=N_KV), :], new_k))
            vparts.append(jnp.where(is_past, vbs[g * SEL_PER_STEP + jj][pl.ds(g, SEL_BLOCK, stride=N_KV), :], new_v))
            blkv = jnp.where(lax.shift_right_logical(lane, int(np.log2(SEL_BLOCK))) == jj, blk, blkv)
        kcat = jnp.concatenate(kparts, axis=0).astype(BF16)
        vcat = jnp.concatenate(vparts, axis=0).astype(BF16)
        rel = blkv * SEL_BLOCK + (lane & (SEL_BLOCK - 1)) - past_len
        s = _dot_t(qs_ref[rows, :].astype(BF16), kcat) * SCALE + slope_col(g) * rel.astype(F32)
        s = jnp.where(rel <= 0, s, NEG_INF)
        m_prev = m_ref[rows]
        m_new = jnp.maximum(m_prev, jnp.max(s, axis=-1, keepdims=True))
        m_safe = jnp.where(m_new == NEG_INF, 0.0, m_new)
        p = jnp.exp(s - m_safe)
        alpha = jnp.exp(m_prev - m_safe)
        l_ref[rows] = alpha * l_ref[rows] + jnp.sum(p, axis=-1, keepdims=True)
        acc_ref[rows] = alpha * acc_ref[rows] + jnp.dot(p.astype(BF16), vcat, preferred_element_type=F32)
        m_ref[rows] = m_new

    @pl.when(jt == N_SEL // SEL_PER_STEP - 1)
    def _():
        i = lax.broadcasted_iota(jnp.int32, (1, w_buf + SUBLANE), 1)
        relw = jnp.minimum(i, w_buf) - w_buf
        validw = (relw > -WINDOW) & (i <= w_buf)
        for g in range(N_KV):
            rows = slice(g * SUBLANE, (g + 1) * SUBLANE)
            o = acc_ref[rows] * (1.0 / jnp.maximum(l_ref[rows], 1e-30))
            new_k = jnp.broadcast_to(kwn_ref[:, g * HEAD_DIM:(g + 1) * HEAD_DIM], (SUBLANE, HEAD_DIM))
            new_v = jnp.broadcast_to(vwn_ref[:, g * HEAD_DIM:(g + 1) * HEAD_DIM], (SUBLANE, HEAD_DIM))
            kw = jnp.concatenate([kwc_ref[pl.ds(g, w_buf, stride=N_KV), :], new_k], axis=0).astype(BF16)
            vw = jnp.concatenate([vwc_ref[pl.ds(g, w_buf, stride=N_KV), :], new_v], axis=0).astype(BF16)
            sw = _dot_t(qs_ref[rows, :].astype(BF16), kw) * SCALE + slope_col(g) * relw.astype(F32)
            ow = jnp.dot(_softmax_rows(sw, validw).astype(BF16), vw, preferred_element_type=F32)
            for r in range(GROUP):
                cols = slice((g * GROUP + r) * HEAD_DIM, (g * GROUP + r + 1) * HEAD_DIM)
                os_ref[:, cols] = _take_row(o, r)
                ow_ref[:, cols] = _take_row(ow, r)
        n_rows = w_buf * N_KV
        rowi = lax.broadcasted_iota(jnp.int32, (n_rows, HEAD_DIM), 0)
        for src, new, dst in ((kwc_ref, kwn_ref, kwo_ref), (vwc_ref, vwn_ref, vwo_ref)):
            out = pltpu.roll(src[...], n_rows - N_KV, 0)
            for g in range(N_KV):
                out = jnp.where(rowi == n_rows - N_KV + g, new[:, g * HEAD_DIM:(g + 1) * HEAD_DIM], out)
            dst[...] = out


def _sample_attn2(idx, page_table, slopes, sproj3, pool_k, pool_v, win_k, win_v, past_len):
    b = sproj3.shape[0]
    w_buf = win_k.shape[0] // (b * N_KV)
    n_blk = past_len // SEL_BLOCK
    bpp = PAGE_SIZE // SEL_BLOCK

    def blk_spec(g, jj):
        def index_map(bb, jt, ix, pt, sl):
            blk = jnp.minimum(ix[bb, g, jt * SEL_PER_STEP + jj], n_blk - 1)
            return (pt[bb, blk // bpp] * bpp + blk % bpp, 0)
        return pl.BlockSpec((BLK_ROWS, HEAD_DIM), index_map)

    blk_specs = [blk_spec(g, jj) for g in range(N_KV) for jj in range(SEL_PER_STEP)]
    row = lambda col, w: pl.BlockSpec((None, 1, w), lambda bb, jt, ix, pt, sl: (bb, 0, col // w))
    win_spec = pl.BlockSpec((w_buf * N_KV, HEAD_DIM), lambda bb, jt, ix, pt, sl: (bb, 0))
    o_spec = pl.BlockSpec((None, 1, ATT_WIDTH), lambda bb, jt, ix, pt, sl: (bb, 0, 0))
    n_q = N_KV * SUBLANE
    return pl.pallas_call(
        functools.partial(_sattn2_kernel, past_len=past_len, w_buf=w_buf),
        grid_spec=pltpu.PrefetchScalarGridSpec(
            num_scalar_prefetch=3,
            grid=(b, N_SEL // SEL_PER_STEP),
            in_specs=[row(C_Q, ATT_WIDTH), row(C_KS, KV_WIDTH), row(C_VS, KV_WIDTH), row(C_KW, KV_WIDTH),
                      row(C_VW, KV_WIDTH)] + blk_specs + blk_specs + [win_spec, win_spec],
            out_specs=[o_spec, o_spec, win_spec, win_spec],
            scratch_shapes=[
                pltpu.VMEM((n_q, HEAD_DIM), F32),
                pltpu.VMEM((n_q, 1), F32),
                pltpu.VMEM((n_q, 1), F32),
                pltpu.VMEM((n_q, HEAD_DIM), F32),
            ],
        ),
        out_shape=[jax.ShapeDtypeStruct((b, 1, ATT_WIDTH), F32)] * 2
        + [jax.ShapeDtypeStruct(win_k.shape, F32)] * 2,
        compiler_params=_cparams(("parallel", "arbitrary")),
        name="sample_attn_sel_win",
    )(idx, page_table, slopes, *([sproj3] * 5), *([pool_k] * (N_KV * SEL_PER_STEP)),
      *([pool_v] * (N_KV * SEL_PER_STEP)), win_k, win_v)


def _merge_kernel(oc_ref, os_ref, ow_ref, gl_ref, za_ref, mgc_ref, mga_ref, cd_ref, x_ref, wa_ref, wo_ref, gp_ref,
                  y_ref):
    gates = _sigmoid(gl_ref[...])
    tm = gates.shape[0]
    pieces = []
    for h in range(N_HEADS):
        cols = slice(h * HEAD_DIM, (h + 1) * HEAD_DIM)
        gate = lambda c: jnp.broadcast_to(gates[:, c * N_HEADS + h:c * N_HEADS + h + 1], (tm, HEAD_DIM))
        pieces.append(gate(0) * oc_ref[:, cols] + gate(1) * os_ref[:, cols] + gate(2) * ow_ref[:, cols])
    o = jnp.concatenate(pieces, axis=1)
    attn_d = jnp.dot((o * _silu(za_ref[...])).astype(BF16), wa_ref[...], preferred_element_type=F32)
    mix = _sigmoid(mgc_ref[...]) * cd_ref[...] + _sigmoid(mga_ref[...]) * attn_d
    z = jnp.dot(mix.astype(BF16), wo_ref[...], preferred_element_type=F32)
    ms = jnp.mean(z * z, axis=-1, keepdims=True)
    y_ref[...] = x_ref[...] + z * lax.rsqrt(ms + RMS_EPS) * gp_ref[...]


def _merge(o_c, o_s, o_w, proj, conv_d, x2d, w_ao, w_o, g_post):
    m = x2d.shape[0]
    tm = min(128, m)
    wide = lambda col: pl.BlockSpec((tm, D_MODEL), lambda i: (i, col // D_MODEL))
    plain = pl.BlockSpec((tm, D_MODEL), lambda i: (i, 0))
    return pl.pallas_call(
        _merge_kernel,
        grid=(m // tm,),
        in_specs=[plain, plain, plain,
                  pl.BlockSpec((tm, LANE), lambda i: (i, C_GL // LANE)),
                  wide(C_ZA), wide(C_MGC), wide(C_MGA), plain, plain,
                  _const_spec((ATT_WIDTH, D_MODEL)), _const_spec((D_MODEL, D_MODEL)), _const_spec((1, D_MODEL))],
        out_specs=plain,
        out_shape=jax.ShapeDtypeStruct((m, D_MODEL), F32),
        compiler_params=_cparams(("parallel",)),
        name="merge",
    )(o_c, o_s, o_w, proj, proj, proj, proj, conv_d, x2d, w_ao, w_o, g_post.reshape(1, D_MODEL))


def _reorder_w_in(w_in):
    idx = [int(v) for v in np.cumsum(IN_SIZES)[:-1]]
    (u, ug, zc, q, kc, vc, ks, vs, kw, vw, za, gl, mgc, mga) = jnp.split(w_in, idx, axis=1)
    pad = jnp.zeros((D_MODEL, PROJ_W - (C_GL + 3 * N_HEADS)), w_in.dtype)
    return jnp.concatenate([u, ug, zc, q, za, mgc, mga, kc, vc, ks, vs, kw, vw, gl, pad], axis=1).astype(BF16)


def _alibi_slopes():
    h = jnp.arange(1, N_HEADS + 1, dtype=F32)
    return jnp.exp2(-8.0 * h / N_HEADS)


def _layer_weights(l, w_in, w_dw, w_conv_out, w_attn_out, w_out):
    w_dw_p = jnp.concatenate([w_dw[l], jnp.zeros((HALO - CONV_K, CONV_C), F32)], axis=0)
    return (_reorder_w_in(w_in[l]), w_dw_p, w_conv_out[l].astype(BF16), w_attn_out[l].astype(BF16),
            w_out[l].astype(BF16))


def _prompt_layer(x, slopes, w_r, w_dw_p, w_co, w_ao, w_o, g_pre, b_dw, ln_g, ln_b, pe_k, w1_k, w2_k, pe_v, w1_v,
                  w2_v, g_post):
    t = x.shape[1]
    x2d = x.reshape(t, D_MODEL)
    proj = _project(x2d, g_pre, w_r)
    conv_d, tail = _conv_branch(proj, w_dw_p, b_dw, ln_g, ln_b, w_co)
    kv = _kv_rows(proj)
    n_pages = t // PAGE_SIZE
    pt = jnp.arange(n_pages, dtype=jnp.int32).reshape(1, n_pages)
    kcs, vcs = _compress(pt, kv[0], kv[1], pe_k, w1_k, w2_k, pe_v, w1_v, w2_v, pps=min(16, n_pages))
    o_c, member, active = _prompt_attn1(slopes, proj, kcs[0], vcs[0])
    o_s = _prompt_sel(slopes, proj, member, active, _sel_expand_matrix_t(t))
    o_w = _prompt_win(slopes, proj)
    y = _merge(o_c, o_s, o_w, proj, conv_d, x2d, w_ao, w_o, g_post)
    wl = min(WINDOW, t)
    heads = lambda a: a.reshape(1, -1, N_KV, HEAD_DIM)
    states = (heads(kv[0]), heads(kv[1]), heads(kv[2]), heads(kv[3]), heads(kv[4][-wl * N_KV:]),
              heads(kv[5][-wl * N_KV:]), tail[-(CONV_K - 1):].reshape(1, CONV_K - 1, CONV_C))
    return y.reshape(1, t, D_MODEL), states


def _sample_layer(x, l, ck_cmp, cv_cmp, ck_slc, cv_slc, ck_win, cv_win, c_conv, page_table, slopes, w_r, w_dw_p,
                  w_co, w_ao, w_o, g_pre, b_dw, ln_g, ln_b, pe_k, w1_k, w2_k, pe_v, w1_v, w2_v, g_post):
    b = x.shape[0]
    n_pool = ck_cmp.shape[1]
    n_pages = page_table.shape[1]
    past_len = n_pages * PAGE_SIZE
    w_buf = ck_win.shape[2]
    x2d = x.reshape(b, D_MODEL)
    sproj = _project(x2d, g_pre, w_r)
    sproj3 = sproj.reshape(b, 1, PROJ_W)
    conv_d, s_conv_t = _sample_conv(jnp.transpose(c_conv[l], (1, 0, 2)), sproj, w_dw_p, b_dw, ln_g, ln_b, w_co)
    s_conv = jnp.transpose(s_conv_t, (1, 0, 2))
    pool = lambda a: a[l].reshape(n_pool * PAGE_ROWS, HEAD_DIM)
    win = lambda a: a[l].reshape(b * w_buf * N_KV, HEAD_DIM)
    kcs, vcs = _compress(page_table, pool(ck_cmp), pool(cv_cmp), pe_k, w1_k, w2_k, pe_v, w1_v, w2_v,
                         pps=min(16, n_pages))
    o_c, idx = _sample_attn1(slopes, sproj3, kcs, vcs, past_len)
    o_s, o_w, kwo, vwo = _sample_attn2(idx[:, :N_KV, :N_SEL], page_table, slopes, sproj3, pool(ck_slc), pool(cv_slc),
                                       win(ck_win), win(cv_win), past_len)
    y = _merge(o_c.reshape(b, ATT_WIDTH), o_s.reshape(b, ATT_WIDTH), o_w.reshape(b, ATT_WIDTH), sproj, conv_d, x2d,
               w_ao, w_o, g_post)
    heads = lambda a: a.reshape(b, -1, N_KV, HEAD_DIM)
    new = _kv_rows(sproj)
    states = (heads(new[0]), heads(new[1]), heads(new[2]), heads(new[3]), heads(kwo), heads(vwo), s_conv)
    return y.reshape(b, 1, D_MODEL), states


def kernel(x_prompt, x_sample, cache_k_cmp, cache_v_cmp, cache_k_slc, cache_v_slc, cache_k_win, cache_v_win,
           cache_conv, page_table, g_pre, w_in, w_dw, b_dw, ln_g, ln_b, w_conv_out, pe_k, w1_k, w2_k, pe_v, w1_v,
           w2_v, w_attn_out, w_out, g_post):
    depth = g_pre.shape[0]
    slopes = _alibi_slopes()
    y_prompt, y_sample = x_prompt, x_sample
    p_states, s_states = [], []
    for l in range(depth):
        mats = _layer_weights(l, w_in, w_dw, w_conv_out, w_attn_out, w_out)
        vecs = (g_pre[l], b_dw[l], ln_g[l], ln_b[l], pe_k[l], w1_k[l], w2_k[l], pe_v[l], w1_v[l], w2_v[l], g_post[l])
        y_prompt, sp = _prompt_layer(y_prompt, slopes, *mats, *vecs)
        y_sample, ss = _sample_layer(y_sample, l, cache_k_cmp, cache_v_cmp, cache_k_slc, cache_v_slc, cache_k_win,
                                     cache_v_win, cache_conv, page_table, slopes, *mats, *vecs)
        p_states.append(sp)
        s_states.append(ss)
    p_out = [jnp.stack(a) for a in zip(*p_states)]
    s_out = [jnp.stack(a) for a in zip(*s_states)]
    return (y_prompt, y_sample, *p_out, *s_out)
```

```python
import functools

import numpy as np
import jax
import jax.numpy as jnp
from jax import lax
from jax.experimental import pallas as pl
from jax.experimental.pallas import tpu as pltpu

F32 = jnp.float32
BF16 = jnp.bfloat16

D_MODEL = 2048
CONV_C = D_MODEL
CONV_K = 31
N_HEADS = 16
N_KV = 4
GROUP = N_HEADS // N_KV
HEAD_DIM = 128
ATT_WIDTH = N_HEADS * HEAD_DIM
KV_WIDTH = N_KV * HEAD_DIM
CMP_LEN = 32
CMP_STRIDE = 16
SEL_BLOCK = 64
N_SEL = 16
N_LOCAL = 2
WINDOW = 512
RMS_EPS = 1e-6
LN_EPS = 1e-5
FORCE_BONUS = 1e9
PAGE_SIZE = 128
IN_SIZES = (CONV_C, CONV_C, CONV_C, ATT_WIDTH) + (KV_WIDTH,) * 6 + (ATT_WIDTH, 3 * N_HEADS, D_MODEL, D_MODEL)

LANE = 128
SUBLANE = 8
NEG_INF = float("-inf")
SCALE = HEAD_DIM ** -0.5

C_U, C_UG, C_ZC, C_Q, C_ZA, C_MGC, C_MGA = (i * D_MODEL for i in range(7))
C_KC = 7 * D_MODEL
C_VC, C_KS, C_VS, C_KW, C_VW = (C_KC + i * KV_WIDTH for i in range(1, 6))
C_GL = C_KC + 6 * KV_WIDTH
PROJ_TN = 512
PROJ_W = C_GL + PROJ_TN
VMEM_LIMIT = 56 * 1024 * 1024


def _cparams(sem):
    return pltpu.CompilerParams(dimension_semantics=sem, vmem_limit_bytes=VMEM_LIMIT)


def _const_spec(shape):
    nd = len(shape)
    return pl.BlockSpec(shape, lambda *a: (0,) * nd, pipeline_mode=pl.Buffered(1))


def _sigmoid(x):
    return jax.nn.sigmoid(x)


def _silu(x):
    return x * jax.nn.sigmoid(x)


def _dot_t(a, b):
    return lax.dot_general(a, b, (((1,), (1,)), ((), ())), preferred_element_type=F32)


def _stack_rows(rows, n):
    w = rows[0].shape[1]
    rid = lax.broadcasted_iota(jnp.int32, (n, 1), 0)
    out = jnp.zeros((n, w), rows[0].dtype)
    for r, v in enumerate(rows):
        out = jnp.where(rid == r, jnp.broadcast_to(v, (n, w)), out)
    return out


def _take_row(x, r):
    rid = lax.broadcasted_iota(jnp.int32, (x.shape[0], 1), 0)
    return jnp.sum(jnp.where(rid == r, x, 0.0), axis=0, keepdims=True)


def _rmsnorm_kernel(x_ref, g_ref, h_ref):
    x = x_ref[...]
    ms = jnp.mean(x * x, axis=-1, keepdims=True)
    h_ref[...] = (x * lax.rsqrt(ms + RMS_EPS) * g_ref[...]).astype(BF16)


def _proj_kernel(a_ix, b_ix, use_b, h_ref, wa_ref, wb_ref, o_ref, w_ref):
    del a_ix, b_ix
    j = pl.program_id(0)

    @pl.when(pl.program_id(1) == 0)
    def _():
        @pl.when(use_b[j] == 0)
        def _():
            w_ref[...] = wa_ref[...].astype(BF16)

        @pl.when(use_b[j] != 0)
        def _():
            w_ref[...] = wb_ref[...].astype(BF16)

    o_ref[...] = _dot_t(h_ref[...], w_ref[...])


PROJ_SEGMENTS = ((C_U, 0), (C_UG, 1), (C_ZC, 2), (C_Q, 3), (C_ZA, 10), (C_MGC, 12), (C_MGA, 13), (C_KC, 4), (C_VC, 5),
                 (C_KS, 6), (C_VS, 7), (C_KW, 8), (C_VW, 9), (C_GL, 11))


def _proj_plan():
    offs = np.concatenate([[0], np.cumsum(IN_SIZES)])
    n_tiles = PROJ_W // PROJ_TN
    a_ix, b_ix, use_b = (np.zeros(n_tiles, np.int32) for _ in range(3))
    b_parts, b_rows = [], 0
    for p_col, seg in PROJ_SEGMENTS:
        lo, hi = int(offs[seg]), int(offs[seg + 1])
        n = -(-(hi - lo) // PROJ_TN)
        aligned = lo % PROJ_TN == 0 and (hi - lo) % PROJ_TN == 0
        for k in range(n):
            tile = p_col // PROJ_TN + k
            if aligned:
                a_ix[tile] = lo // PROJ_TN + k
            else:
                use_b[tile], b_ix[tile] = 1, b_rows // PROJ_TN + k
        if not aligned:
            b_parts.append((lo, hi, n * PROJ_TN - (hi - lo)))
            b_rows += n * PROJ_TN
    for tile in range(1, n_tiles):
        if use_b[tile]:
            a_ix[tile] = a_ix[tile - 1]
        else:
            b_ix[tile] = b_ix[tile - 1]
    return a_ix, b_ix, use_b, b_parts


def _proj_weights(w_in_l):
    w_t = jnp.swapaxes(w_in_l, 0, 1)
    parts = []
    for lo, hi, pad in _proj_plan()[3]:
        parts.append(w_t[lo:hi])
        if pad:
            parts.append(jnp.zeros((pad, D_MODEL), w_t.dtype))
    return w_t, jnp.concatenate(parts, axis=0)


def _project(x2d, g_pre, w_t, w_side):
    m = x2d.shape[0]
    tm = min(1024, m)
    h = pl.pallas_call(
        _rmsnorm_kernel,
        grid=(m // tm,),
        in_specs=[pl.BlockSpec((tm, D_MODEL), lambda i: (i, 0)), pl.BlockSpec((1, D_MODEL), lambda i: (0, 0))],
        out_specs=pl.BlockSpec((tm, D_MODEL), lambda i: (i, 0)),
        out_shape=jax.ShapeDtypeStruct((m, D_MODEL), BF16),
        compiler_params=_cparams(("parallel",)),
        name="rmsnorm",
    )(x2d, g_pre.reshape(1, D_MODEL))
    a_ix, b_ix, use_b, _ = _proj_plan()
    return pl.pallas_call(
        _proj_kernel,
        grid_spec=pltpu.PrefetchScalarGridSpec(
            num_scalar_prefetch=3,
            grid=(PROJ_W // PROJ_TN, m // tm),
            in_specs=[
                pl.BlockSpec((tm, D_MODEL), lambda j, i, a, b, u: (i, 0)),
                pl.BlockSpec((PROJ_TN, D_MODEL), lambda j, i, a, b, u: (a[j], 0)),
                pl.BlockSpec((PROJ_TN, D_MODEL), lambda j, i, a, b, u: (b[j], 0)),
            ],
            out_specs=pl.BlockSpec((tm, PROJ_TN), lambda j, i, a, b, u: (i, j)),
            scratch_shapes=[pltpu.VMEM((PROJ_TN, D_MODEL), BF16)],
        ),
        out_shape=jax.ShapeDtypeStruct((m, PROJ_W), F32),
        compiler_params=_cparams(("arbitrary", "arbitrary")),
        name="proj",
    )(jnp.asarray(a_ix), jnp.asarray(b_ix), jnp.asarray(use_b), h, w_t, w_side)


KV_COLS = (C_KC, C_VC, C_KS, C_VS, C_KW, C_VW)


def _kv_rows_kernel(*refs):
    n = len(refs) // 2
    for src, dst in zip(refs[:n], refs[n:]):
        tm = src.shape[0]
        for g in range(N_KV):
            dst[pl.ds(g, tm, stride=N_KV), :] = src[:, g * HEAD_DIM:(g + 1) * HEAD_DIM]


def _kv_rows(proj):
    m = proj.shape[0]
    tm = min(512, m)
    col_spec = lambda c: pl.BlockSpec((tm, KV_WIDTH), lambda i: (i, c // KV_WIDTH))
    return pl.pallas_call(
        _kv_rows_kernel,
        grid=(m // tm,),
        in_specs=[col_spec(c) for c in KV_COLS],
        out_specs=[pl.BlockSpec((tm * N_KV, HEAD_DIM), lambda i: (i, 0))] * len(KV_COLS),
        out_shape=[jax.ShapeDtypeStruct((m * N_KV, HEAD_DIM), F32)] * len(KV_COLS),
        compiler_params=_cparams(("parallel",)),
        name="kv_rows",
    )(*([proj] * len(KV_COLS)))


HALO = 32
CONV_RB = 8


def _ln_gate_out(c, zc, bdw, lng, lnb, wco_ref):
    c = c + bdw
    mean = jnp.mean(c, axis=-1, keepdims=True)
    xc = c - mean
    var = jnp.mean(xc * xc, axis=-1, keepdims=True)
    y = xc * lax.rsqrt(var + LN_EPS) * lng + lnb
    act = _silu(y) * _silu(zc)
    return jnp.dot(act.astype(BF16), wco_ref[...], preferred_element_type=F32)


def _conv_kernel(u_ref, ug_ref, zc_ref, uh_ref, ugh_ref, wdw_ref, bdw_ref, lng_ref, lnb_ref, wco_ref,
                 out_ref, tail_ref, uext_ref, sh_ref, c_ref):
    i = pl.program_id(0)
    tm = u_ref.shape[0]
    glu = u_ref[...] * _sigmoid(ug_ref[...])
    halo = uh_ref[...] * _sigmoid(ugh_ref[...])
    not_first = (jnp.zeros((HALO, 1), jnp.int32) + i) > 0
    uext_ref[0:HALO, :] = jnp.where(not_first, halo, 0.0)
    uext_ref[HALO:HALO + tm, :] = glu
    tail_ref[...] = glu[tm - HALO:, :]
    n_sh = tm + HALO - SUBLANE
    for p in range(1, SUBLANE):
        sh_ref[p - 1] = uext_ref[p:p + n_sh, :]

    off = HALO - (CONV_K - 1)

    def body(rc, carry):
        r0 = pl.multiple_of(rc * CONV_RB, CONV_RB)
        acc = jnp.zeros((CONV_RB, CONV_C), F32)
        for k in range(CONV_K):
            p, base = (off + k) % SUBLANE, (off + k) // SUBLANE * SUBLANE
            if p == 0:
                win = uext_ref[pl.ds(r0 + base, CONV_RB), :]
            else:
                win = sh_ref[p - 1, pl.ds(r0 + base, CONV_RB), :]
            acc = acc + win * wdw_ref[k]
        c_ref[pl.ds(r0, CONV_RB), :] = acc
        return carry

    lax.fori_loop(0, tm // CONV_RB, body, 0)
    out_ref[...] = _ln_gate_out(c_ref[...], zc_ref[...], bdw_ref[...], lng_ref[...], lnb_ref[...], wco_ref)


def _conv_branch(proj, w_dw_p, b_dw, ln_g, ln_b, w_co):
    t = proj.shape[0]
    tm = min(256, t)
    hb = tm // HALO
    row = lambda a: a.reshape(1, CONV_C)
    return pl.pallas_call(
        _conv_kernel,
        grid=(t // tm,),
        in_specs=[
            pl.BlockSpec((tm, CONV_C), lambda i: (i, C_U // CONV_C)),
            pl.BlockSpec((tm, CONV_C), lambda i: (i, C_UG // CONV_C)),
            pl.BlockSpec((tm, CONV_C), lambda i: (i, C_ZC // CONV_C)),
            pl.BlockSpec((HALO, CONV_C), lambda i: (jnp.maximum(i * hb - 1, 0), C_U // CONV_C)),
            pl.BlockSpec((HALO, CONV_C), lambda i: (jnp.maximum(i * hb - 1, 0), C_UG // CONV_C)),
            _const_spec((CONV_K, CONV_RB, CONV_C)),
            _const_spec((1, CONV_C)),
            _const_spec((1, CONV_C)),
            _const_spec((1, CONV_C)),
            _const_spec((CONV_C, D_MODEL)),
        ],
        out_specs=[
            pl.BlockSpec((tm, D_MODEL), lambda i: (i, 0)),
            pl.BlockSpec((HALO, CONV_C), lambda i: (0, 0)),
        ],
        out_shape=[jax.ShapeDtypeStruct((t, D_MODEL), F32), jax.ShapeDtypeStruct((HALO, CONV_C), F32)],
        scratch_shapes=[pltpu.VMEM((tm + HALO, CONV_C), F32),
                        pltpu.VMEM((SUBLANE - 1, tm + HALO - SUBLANE, CONV_C), F32),
                        pltpu.VMEM((tm, CONV_C), F32)],
        compiler_params=_cparams(("arbitrary",)),
        name="conv_branch",
    )(proj, proj, proj, proj, proj,
      jnp.broadcast_to(w_dw_p[:CONV_K, None, :], (CONV_K, CONV_RB, CONV_C)),
      row(b_dw), row(ln_g), row(ln_b), w_co)


def _sconv_kernel(cc_ref, u_ref, ug_ref, zc_ref, wdw_ref, bdw_ref, lng_ref, lnb_ref, wco_ref,
                  out_ref, state_ref):
    nh = CONV_K - 1
    glu = u_ref[...] * _sigmoid(ug_ref[...])
    c = glu * wdw_ref[nh:nh + 1, :]
    for k in range(nh):
        c = c + cc_ref[k] * wdw_ref[k:k + 1, :]
    out_ref[...] = _ln_gate_out(c, zc_ref[...], bdw_ref[...], lng_ref[...], lnb_ref[...], wco_ref)
    for k in range(nh - 1):
        state_ref[k] = cc_ref[k + 1]
    state_ref[nh - 1] = glu


def _sample_conv(cc, sproj, w_dw_p, b_dw, ln_g, ln_b, w_co):
    nh, b = cc.shape[0], cc.shape[1]
    row = lambda a: a.reshape(1, CONV_C)
    return pl.pallas_call(
        _sconv_kernel,
        grid=(1,),
        in_specs=[
            pl.BlockSpec((nh, b, CONV_C), lambda i: (0, 0, 0)),
            pl.BlockSpec((b, CONV_C), lambda i: (0, C_U // CONV_C)),
            pl.BlockSpec((b, CONV_C), lambda i: (0, C_UG // CONV_C)),
            pl.BlockSpec((b, CONV_C), lambda i: (0, C_ZC // CONV_C)),
            pl.BlockSpec((HALO, CONV_C), lambda i: (0, 0)),
            pl.BlockSpec((1, CONV_C), lambda i: (0, 0)),
            pl.BlockSpec((1, CONV_C), lambda i: (0, 0)),
            pl.BlockSpec((1, CONV_C), lambda i: (0, 0)),
            pl.BlockSpec((CONV_C, D_MODEL), lambda i: (0, 0)),
        ],
        out_specs=[
            pl.BlockSpec((b, D_MODEL), lambda i: (0, 0)),
            pl.BlockSpec((nh, b, CONV_C), lambda i: (0, 0, 0)),
        ],
        out_shape=[jax.ShapeDtypeStruct((b, D_MODEL), F32), jax.ShapeDtypeStruct((nh, b, CONV_C), F32)],
        compiler_params=_cparams(("arbitrary",)),
        name="sample_conv",
    )(cc, sproj, sproj, sproj, w_dw_p, row(b_dw), row(ln_g), row(ln_b), w_co)


SUB_PER_PAGE = PAGE_SIZE // CMP_STRIDE
PAGE_ROWS = PAGE_SIZE * N_KV
R_N = CMP_LEN // CMP_STRIDE
STEP_PERM = tuple(range(0, CMP_STRIDE, 2)) + tuple(range(1, CMP_STRIDE, 2))
HALF_SUB = SUBLANE // 2


def _compress_kernel(pt_ref, *refs, pps):
    del pt_ref
    pages = (refs[:pps], refs[pps:2 * pps])
    w01s = refs[2 * pps:2 * pps + 2]
    pes = refs[2 * pps + 2:2 * pps + 4]
    w2s = refs[2 * pps + 4:2 * pps + 6]
    outs = refs[2 * pps + 6:2 * pps + 8]
    carries = refs[2 * pps + 8:2 * pps + 10]
    xs_refs = refs[2 * pps + 10:2 * pps + 12]
    rows = pps * SUB_PER_PAGE * N_KV
    n_c = CMP_STRIDE // 2
    grp = n_c * SUBLANE

    @pl.when(pl.program_id(1) == 0)
    def _():
        carries[0][...] = jnp.zeros_like(carries[0])
        carries[1][...] = jnp.zeros_like(carries[1])

    top3 = lax.broadcasted_iota(jnp.int32, (n_c, SUBLANE, HEAD_DIM), 1) < HALF_SUB
    top = lax.broadcasted_iota(jnp.int32, (SUBLANE, HEAD_DIM), 0) < HALF_SUB
    for which in range(2):
        xs_ref = xs_refs[which]
        for i, pg in enumerate(pages[which]):
            los, his = [], []
            for p in range(SUB_PER_PAGE // 2):
                a = pg[(2 * p) * grp:(2 * p + 1) * grp, :].reshape(n_c, SUBLANE, HEAD_DIM)
                b = pg[(2 * p + 1) * grp:(2 * p + 2) * grp, :].reshape(n_c, SUBLANE, HEAD_DIM)
                los.append(jnp.where(top3, a, pltpu.roll(b, HALF_SUB, 1)))
                his.append(jnp.where(top3, pltpu.roll(a, HALF_SUB, 1), b))
            for pp in range(len(los) // 2):
                r0 = (i * (SUB_PER_PAGE // 2) + 2 * pp) * SUBLANE
                for c in range(n_c):
                    for half, parts in ((0, los), (1, his)):
                        col = (half * n_c + c) * HEAD_DIM
                        xs_ref[r0:r0 + 2 * SUBLANE, col:col + HEAD_DIM] = jnp.concatenate(
                            [parts[2 * pp][c], parts[2 * pp + 1][c]], axis=0).astype(BF16)
        w01 = w01s[which][...]
        a = jnp.dot(xs_ref[...], w01, preferred_element_type=F32)
        pe_o = jnp.dot(pes[which][...].astype(BF16), w01, preferred_element_type=F32)
        hid0 = pe_o[0:1, :HEAD_DIM] + pe_o[1:2, HEAD_DIM:]
        carry = carries[which]
        a0, a1 = a[:, :HEAD_DIM], a[:, HEAD_DIM:]
        a0s = pltpu.roll(a0, N_KV, 0)
        first = jnp.where(top, carry[...], a0s[0:SUBLANE])
        a0s = jnp.concatenate([first, a0s[SUBLANE:]], axis=0)
        carry[...] = pltpu.roll(a0[rows - SUBLANE:rows, :], N_KV, 0)
        hid = hid0 + a0s + a1
        outs[which][...] = jnp.dot(_silu(hid).astype(BF16), w2s[which][...], preferred_element_type=F32)


def _w01(w1):
    w = w1.reshape(R_N, CMP_STRIDE, HEAD_DIM, HEAD_DIM)[:, STEP_PERM, :, :]
    w = w.reshape(R_N, CMP_STRIDE * HEAD_DIM, HEAD_DIM)
    return jnp.concatenate([w[r] for r in range(R_N)], axis=1).astype(BF16)


def _pe2(pe):
    p = pe.reshape(R_N, CMP_STRIDE, HEAD_DIM)[:, STEP_PERM, :].reshape(R_N, CMP_STRIDE * HEAD_DIM)
    return jnp.concatenate([p, jnp.zeros((SUBLANE - R_N, CMP_STRIDE * HEAD_DIM), F32)], axis=0)


def _compress(page_table, cache_k, cache_v, pe_k, w1_k, w2_k, pe_v, w1_v, w2_v, pps):
    b, n_pages = page_table.shape
    rows = pps * SUB_PER_PAGE * N_KV
    n_sub = n_pages * SUB_PER_PAGE

    def page_spec(i):
        return pl.BlockSpec((PAGE_ROWS, HEAD_DIM), lambda bb, c, pt: (pt[bb, c * pps + i], 0))

    cspec = lambda shape: pl.BlockSpec(shape, lambda bb, c, pt: (0,) * len(shape))
    in_specs = [page_spec(i) for i in range(pps)] * 2 + [
        cspec((CMP_STRIDE * HEAD_DIM, R_N * HEAD_DIM))] * 2 + [
        cspec((SUBLANE, CMP_STRIDE * HEAD_DIM))] * 2 + [cspec((HEAD_DIM, HEAD_DIM))] * 2
    out_spec = pl.BlockSpec((None, rows, HEAD_DIM), lambda bb, c, pt: (bb, c, 0))
    return pl.pallas_call(
        functools.partial(_compress_kernel, pps=pps),
        grid_spec=pltpu.PrefetchScalarGridSpec(
            num_scalar_prefetch=1,
            grid=(b, n_pages // pps),
            in_specs=in_specs,
            out_specs=[out_spec, out_spec],
            scratch_shapes=[pltpu.VMEM((SUBLANE, HEAD_DIM), F32)] * 2
            + [pltpu.VMEM((rows, CMP_STRIDE * HEAD_DIM), BF16)] * 2,
        ),
        out_shape=[jax.ShapeDtypeStruct((b, n_sub * N_KV, HEAD_DIM), F32)] * 2,
        compiler_params=_cparams(("parallel", "arbitrary")),
        name="compress",
    )(page_table, *([cache_k] * pps), *([cache_v] * pps), _w01(w1_k), _w01(w1_v), _pe2(pe_k), _pe2(pe_v),
      w2_k.astype(BF16), w2_v.astype(BF16))


def _ovl_shifted(n_sub, n_lanes):
    i = (np.arange(n_sub)[:, None] - 1) * CMP_STRIDE
    j = np.arange(n_lanes)[None, :] * SEL_BLOCK
    ov = np.clip(np.minimum(i + CMP_LEN, j + SEL_BLOCK) - np.maximum(i, j), 0, None).astype(np.float32) / CMP_LEN
    ov[0, :] = 0.0
    return jnp.asarray(ov, dtype=BF16)


def _softmax_rows(s, valid):
    s = jnp.where(valid, s, NEG_INF)
    mx = jnp.max(s, axis=-1, keepdims=True)
    mx = jnp.where(mx == NEG_INF, 0.0, mx)
    p = jnp.exp(s - mx)
    den = jnp.maximum(jnp.sum(p, axis=-1, keepdims=True), 1e-30)
    return p * (1.0 / den)


def _select_blocks(imp, qpos, n_sel_blocks):
    r, l = imp.shape
    lane = lax.broadcasted_iota(jnp.int32, (r, l), 1)
    lanef = lane.astype(F32)
    cur = lax.shift_right_logical(qpos, int(np.log2(SEL_BLOCK)))
    valid = lane * SEL_BLOCK <= qpos
    forced = (lane == 0) | ((lane <= cur) & (lane > cur - N_LOCAL))
    score = jnp.where(valid, imp + jnp.where(forced, FORCE_BONUS, 0.0), -FORCE_BONUS)
    score = jnp.where(lane < n_sel_blocks, score, -3e38)
    member = jnp.zeros((r, l), F32)
    picks = []
    for _ in range(N_SEL):
        mx = jnp.max(score, axis=-1, keepdims=True)
        idx = jnp.min(jnp.where(score == mx, lanef, float(l)), axis=-1, keepdims=True)
        hit = lanef == idx
        member = jnp.where(hit, 1.0, member)
        score = jnp.where(hit, NEG_INF, score)
        picks.append(idx)
    return picks, member


def _attn1_kernel(sl_ref, q_ref, kc_ref, vc_ref, ovl_ref, tile_ref, oc_ref, mem_ref, act_ref, *, tq, n_sub,
                  n_sel_blocks):
    qi = pl.program_id(0)
    qpos = qi * tq + lax.broadcasted_iota(jnp.int32, (tq, 1), 0)
    m = lax.broadcasted_iota(jnp.int32, (1, n_sub), 1)
    c_end = (m - 1) * CMP_STRIDE + (CMP_LEN - 1)
    c_ctr = ((m - 1) * CMP_STRIDE).astype(F32) + (CMP_LEN - 1) / 2
    valid = (m >= 1) & (c_end <= qpos)
    dist = jnp.abs(qpos.astype(F32) - c_ctr)
    any_blk = []
    for g in range(N_KV):
        qg = jnp.concatenate([q_ref[:, (g * GROUP + r) * HEAD_DIM:(g * GROUP + r + 1) * HEAD_DIM]
                              for r in range(GROUP)], axis=0).astype(BF16)
        kg = kc_ref[pl.ds(g, n_sub, stride=N_KV), :].astype(BF16)
        s = _dot_t(qg, kg)
        ps = []
        for r in range(GROUP):
            sr = s[r * tq:(r + 1) * tq] * SCALE - sl_ref[g * GROUP + r] * dist
            ps.append(_softmax_rows(sr, valid))
        psum = ps[0]
        for r in range(1, GROUP):
            psum = psum + ps[r]
        pcat = jnp.concatenate(ps, axis=0).astype(BF16)
        oc = jnp.dot(pcat, vc_ref[pl.ds(g, n_sub, stride=N_KV), :].astype(BF16), preferred_element_type=F32)
        for r in range(GROUP):
            oc_ref[:, (g * GROUP + r) * HEAD_DIM:(g * GROUP + r + 1) * HEAD_DIM] = oc[r * tq:(r + 1) * tq]
        imp = jnp.dot(psum.astype(BF16), ovl_ref[...], preferred_element_type=F32)
        _, member = _select_blocks(imp, qpos, n_sel_blocks)
        mem_ref[g] = member.astype(BF16)
        any_blk.append(jnp.max(member, axis=0, keepdims=True))
    hits = jnp.dot(_stack_rows(any_blk, SUBLANE).astype(BF16), tile_ref[...], preferred_element_type=F32)
    act_ref[...] = (hits > 0.5).astype(jnp.int32)


def _prompt_attn1(slopes, proj, kcs, vcs):
    t = proj.shape[0]
    n_sub = kcs.shape[0] // N_KV
    tq = min(ATTN1_TQ, t)
    n_sel_blocks = -(-t // SEL_BLOCK)
    blocks_per_tile = min(SEL_TK, t) // SEL_BLOCK
    assert n_sel_blocks <= LANE and n_sel_blocks <= blocks_per_tile * N_KT_LANES
    ovl = _ovl_shifted(n_sub, LANE)
    blk = np.arange(LANE)[:, None]
    tile_of_block = jnp.asarray((blk // blocks_per_tile == np.arange(LANE)[None, :]) & (blk != 0), dtype=BF16)
    o_c, member, act = pl.pallas_call(
        functools.partial(_attn1_kernel, tq=tq, n_sub=n_sub, n_sel_blocks=n_sel_blocks),
        grid_spec=pltpu.PrefetchScalarGridSpec(
            num_scalar_prefetch=1,
            grid=(t // tq,),
            in_specs=[
                pl.BlockSpec((tq, ATT_WIDTH), lambda i, sl: (i, C_Q // ATT_WIDTH)),
                pl.BlockSpec((n_sub * N_KV, HEAD_DIM), lambda i, sl: (0, 0)),
                pl.BlockSpec((n_sub * N_KV, HEAD_DIM), lambda i, sl: (0, 0)),
                pl.BlockSpec((n_sub, LANE), lambda i, sl: (0, 0)),
                pl.BlockSpec((LANE, LANE), lambda i, sl: (0, 0)),
            ],
            out_specs=[
                pl.BlockSpec((tq, ATT_WIDTH), lambda i, sl: (i, 0)),
                pl.BlockSpec((N_KV, tq, LANE), lambda i, sl: (0, i, 0)),
                pl.BlockSpec((SUBLANE, LANE), lambda i, sl: (i, 0)),
            ],
        ),
        out_shape=[jax.ShapeDtypeStruct((t, ATT_WIDTH), F32), jax.ShapeDtypeStruct((N_KV, t, LANE), BF16),
                   jax.ShapeDtypeStruct((t // tq * SUBLANE, LANE), jnp.int32)],
        compiler_params=_cparams(("parallel",)),
        name="prompt_attn_cmp",
    )(slopes, proj, kcs, vcs, ovl, tile_of_block)
    return o_c, member, act[:, :N_KT_LANES].reshape(-1)


LOG2E = float(np.log2(np.e))


SEL_TQ = 256
SEL_TK = 1024
ATTN1_TQ = 256
N_KT_LANES = LANE // (SEL_TK // SEL_BLOCK)


def _stack_heads(q_ref, tq):
    return jnp.concatenate([(q_ref[:, r * HEAD_DIM:(r + 1) * HEAD_DIM] * (SCALE * LOG2E)).astype(BF16)
                            for r in range(GROUP)], axis=0)


def _sel_kernel(sl_ref, qt_ref, kt_ref, fl_ref, act_ref, q_ref, k_ref, v_ref, mem_ref, et_ref, o_ref,
                qs_ref, m_ref, l_ref, acc_ref, *, tq, tk):
    g = pl.program_id(0)
    step = pl.program_id(1)
    qi = qt_ref[step]
    kt = kt_ref[step]
    flags = fl_ref[step]

    @pl.when((flags & 1) != 0)
    def _():
        qs_ref[...] = _stack_heads(q_ref, tq)
        m_ref[...] = jnp.full(m_ref.shape, NEG_INF, F32)
        l_ref[...] = jnp.zeros(l_ref.shape, F32)
        acc_ref[...] = jnp.zeros(acc_ref.shape, F32)

    sub = tq // ATTN1_TQ
    active = act_ref[((qi * sub) * SUBLANE + g) * N_KT_LANES + kt]
    for h in range(1, sub):
        active = active | act_ref[((qi * sub + h) * SUBLANE + g) * N_KT_LANES + kt]

    def attend(n):
        st = _dot_t(k_ref[0:n, :].astype(BF16), qs_ref[...])
        kpos = kt * tk + lax.broadcasted_iota(jnp.int32, (n, 1), 0)
        qpos = qi * tq + lax.broadcasted_iota(jnp.int32, (1, tq), 1)
        mask = (_dot_t(et_ref[0:n, :], mem_ref[...]) > 0.5) & (kpos <= qpos)
        koff = (kpos - qi * tq).astype(F32)
        vt = v_ref[0:n, :].T.astype(BF16)
        for r in range(GROUP):
            cols = slice(r * tq, (r + 1) * tq)
            sr = jnp.where(mask, st[:, cols] + (sl_ref[g * GROUP + r] * LOG2E) * koff, NEG_INF)
            m_prev = m_ref[:, cols]
            m_new = jnp.maximum(m_prev, jnp.max(sr, axis=0, keepdims=True))
            m_safe = jnp.where(m_new == NEG_INF, 0.0, m_new)
            p = jnp.exp2(sr - m_safe)
            alpha = jnp.exp2(m_prev - m_safe)
            l_ref[:, cols] = alpha * l_ref[:, cols] + jnp.sum(p, axis=0, keepdims=True)
            acc_ref[:, cols] = alpha * acc_ref[:, cols] + jnp.dot(vt, p.astype(BF16), preferred_element_type=F32)
            m_ref[:, cols] = m_new

    @pl.when(active != 0)
    def _():
        attend(tk)

    @pl.when((active == 0) & (kt == 0))
    def _():
        attend(LANE)

    @pl.when((flags & 2) != 0)
    def _():
        for r in range(GROUP):
            cols = slice(r * tq, (r + 1) * tq)
            o = acc_ref[:, cols] * (1.0 / jnp.maximum(l_ref[:, cols], 1e-30))
            o_ref[:, r * HEAD_DIM:(r + 1) * HEAD_DIM] = o.T


def _sel_steps(t, tq, tk):
    qt, kt, fl = [], [], []
    for qi in range(t // tq):
        hi = (qi * tq + tq - 1) // tk
        for k in range(hi + 1):
            qt.append(qi)
            kt.append(k)
            fl.append((1 if k == 0 else 0) | (2 if k == hi else 0))
    as_i32 = lambda a: jnp.asarray(np.asarray(a, np.int32))
    return as_i32(qt), as_i32(kt), as_i32(fl)


def _prompt_sel(slopes, proj, member, active, emat_t):
    t = proj.shape[0]
    tq, tk = min(SEL_TQ, t), min(SEL_TK, t)
    qt, kt, fl = _sel_steps(t, tq, tk)
    imap = lambda f: (lambda g, s, sl, qt, kt, fl, act: f(g, qt[s], kt[s]))
    return pl.pallas_call(
        functools.partial(_sel_kernel, tq=tq, tk=tk),
        grid_spec=pltpu.PrefetchScalarGridSpec(
            num_scalar_prefetch=5,
            grid=(N_KV, int(qt.shape[0])),
            in_specs=[
                pl.BlockSpec((tq, GROUP * HEAD_DIM), imap(lambda g, qi, ki: (qi, C_Q // (GROUP * HEAD_DIM) + g))),
                pl.BlockSpec((tk, HEAD_DIM), imap(lambda g, qi, ki: (ki, C_KS // HEAD_DIM + g))),
                pl.BlockSpec((tk, HEAD_DIM), imap(lambda g, qi, ki: (ki, C_VS // HEAD_DIM + g))),
                pl.BlockSpec((None, tq, LANE), imap(lambda g, qi, ki: (g, qi, 0))),
                pl.BlockSpec((tk, LANE), imap(lambda g, qi, ki: (ki, 0))),
            ],
            out_specs=pl.BlockSpec((tq, GROUP * HEAD_DIM), imap(lambda g, qi, ki: (qi, g))),
            scratch_shapes=[
                pltpu.VMEM((GROUP * tq, HEAD_DIM), BF16),
                pltpu.VMEM((1, GROUP * tq), F32),
                pltpu.VMEM((1, GROUP * tq), F32),
                pltpu.VMEM((HEAD_DIM, GROUP * tq), F32),
            ],
        ),
        out_shape=jax.ShapeDtypeStruct((t, ATT_WIDTH), F32),
        compiler_params=_cparams(("parallel", "arbitrary")),
        name="prompt_attn_sel",
    )(slopes, qt, kt, fl, active, proj, proj, proj, member, emat_t)


def _win_kernel(sl_ref, q_ref, *refs, tq, n_kt):
    k_refs, v_refs, o_ref = refs[:n_kt], refs[n_kt:2 * n_kt], refs[2 * n_kt]
    g = pl.program_id(0)
    qi = pl.program_id(1)
    qs = _stack_heads(q_ref, tq)
    k = jnp.concatenate([kr[...] for kr in k_refs], axis=0).astype(BF16)
    vt = jnp.concatenate([vr[...] for vr in v_refs], axis=0).T.astype(BF16)
    st = _dot_t(k, qs)
    kpos = (qi - (n_kt - 1)) * tq + lax.broadcasted_iota(jnp.int32, (n_kt * tq, 1), 0)
    qpos = qi * tq + lax.broadcasted_iota(jnp.int32, (1, tq), 1)
    rel = kpos - qpos
    mask = (rel <= 0) & (rel > -WINDOW) & (kpos >= 0)
    koff = (kpos - qi * tq).astype(F32)
    for r in range(GROUP):
        sr = jnp.where(mask, st[:, r * tq:(r + 1) * tq] + (sl_ref[g * GROUP + r] * LOG2E) * koff, NEG_INF)
        mx = jnp.max(sr, axis=0, keepdims=True)
        p = jnp.exp2(sr - jnp.where(mx == NEG_INF, 0.0, mx))
        den = jnp.maximum(jnp.sum(p, axis=0, keepdims=True), 1e-30)
        o = jnp.dot(vt, p.astype(BF16), preferred_element_type=F32) * (1.0 / den)
        o_ref[:, r * HEAD_DIM:(r + 1) * HEAD_DIM] = o.T


def _prompt_win(slopes, proj):
    t = proj.shape[0]
    tq = min(SEL_TQ, t)
    n_kt = -(-(WINDOW - 1) // tq) + 1
    kv_spec = lambda col, j: pl.BlockSpec(
        (tq, HEAD_DIM), lambda g, qi, sl: (jnp.maximum(qi - (n_kt - 1) + j, 0), col // HEAD_DIM + g))
    return pl.pallas_call(
        functools.partial(_win_kernel, tq=tq, n_kt=n_kt),
        grid_spec=pltpu.PrefetchScalarGridSpec(
            num_scalar_prefetch=1,
            grid=(N_KV, t // tq),
            in_specs=[pl.BlockSpec((tq, GROUP * HEAD_DIM), lambda g, qi, sl: (qi, C_Q // (GROUP * HEAD_DIM) + g))]
            + [kv_spec(C_KW, j) for j in range(n_kt)] + [kv_spec(C_VW, j) for j in range(n_kt)],
            out_specs=pl.BlockSpec((tq, GROUP * HEAD_DIM), lambda g, qi, sl: (qi, g)),
        ),
        out_shape=jax.ShapeDtypeStruct((t, ATT_WIDTH), F32),
        compiler_params=_cparams(("parallel", "parallel")),
        name="prompt_attn_win",
    )(slopes, *([proj] * (1 + 2 * n_kt)))


def _sel_expand_matrix_t(t):
    e = (np.arange(t)[:, None] // SEL_BLOCK) == np.arange(LANE)[None, :]
    return jnp.asarray(e, dtype=BF16)


def _sattn1_kernel(sl_ref, q_ref, kc_ref, vc_ref, ovl_ref, oc_ref, idx_ref, *, n_sub, qpos, n_sel_blocks, n_lanes):
    m = lax.broadcasted_iota(jnp.int32, (1, n_sub), 1)
    c_end = (m - 1) * CMP_STRIDE + (CMP_LEN - 1)
    c_ctr = ((m - 1) * CMP_STRIDE).astype(F32) + (CMP_LEN - 1) / 2
    valid = (m >= 1) & (c_end <= qpos)
    dist = jnp.abs(float(qpos) - c_ctr)
    qposv = jnp.full((SUBLANE, 1), qpos, jnp.int32)
    out_lane = lax.broadcasted_iota(jnp.int32, (SUBLANE, LANE), 1)
    idx_acc = jnp.zeros((SUBLANE, LANE), F32)
    head_row = lax.broadcasted_iota(jnp.int32, (SUBLANE, 1), 0)
    psums = []
    for g in range(N_KV):
        qg = _stack_rows([q_ref[:, (g * GROUP + r) * HEAD_DIM:(g * GROUP + r + 1) * HEAD_DIM]
                          for r in range(GROUP)], SUBLANE).astype(BF16)
        s = _dot_t(qg, kc_ref[pl.ds(g, n_sub, stride=N_KV), :].astype(BF16))
        slope = jnp.zeros((SUBLANE, 1), F32)
        for r in range(GROUP):
            slope = jnp.where(head_row == r, sl_ref[g * GROUP + r], slope)
        p = _softmax_rows(s * SCALE - slope * dist, valid)
        p = jnp.where(head_row < GROUP, p, 0.0)
        oc = jnp.dot(p.astype(BF16), vc_ref[pl.ds(g, n_sub, stride=N_KV), :].astype(BF16),
                     preferred_element_type=F32)
        for r in range(GROUP):
            oc_ref[:, (g * GROUP + r) * HEAD_DIM:(g * GROUP + r + 1) * HEAD_DIM] = _take_row(oc, r)
        psums.append(jnp.sum(p, axis=0, keepdims=True))
    imp = jnp.dot(_stack_rows(psums, SUBLANE).astype(BF16), ovl_ref[...], preferred_element_type=F32)
    picks, _ = _select_blocks(imp, qposv, n_sel_blocks)
    for j, pk in enumerate(picks):
        idx_acc = jnp.where(out_lane == j, pk, idx_acc)
    idx_ref[...] = idx_acc.astype(jnp.int32)


def _sample_attn1(slopes, sproj3, kcs, vcs, past_len):
    b = sproj3.shape[0]
    n_sub = kcs.shape[1] // N_KV
    n_sel_blocks = -(-(past_len + 1) // SEL_BLOCK)
    n_lanes = -(-n_sel_blocks // LANE) * LANE
    ovl = _ovl_shifted(n_sub, n_lanes)
    return pl.pallas_call(
        functools.partial(_sattn1_kernel, n_sub=n_sub, qpos=past_len, n_sel_blocks=n_sel_blocks, n_lanes=n_lanes),
        grid_spec=pltpu.PrefetchScalarGridSpec(
            num_scalar_prefetch=1,
            grid=(b,),
            in_specs=[
                pl.BlockSpec((None, 1, ATT_WIDTH), lambda i, sl: (i, 0, C_Q // ATT_WIDTH)),
                pl.BlockSpec((None, n_sub * N_KV, HEAD_DIM), lambda i, sl: (i, 0, 0)),
                pl.BlockSpec((None, n_sub * N_KV, HEAD_DIM), lambda i, sl: (i, 0, 0)),
                pl.BlockSpec((n_sub, n_lanes), lambda i, sl: (0, 0)),
            ],
            out_specs=[
                pl.BlockSpec((None, 1, ATT_WIDTH), lambda i, sl: (i, 0, 0)),
                pl.BlockSpec((None, SUBLANE, LANE), lambda i, sl: (i, 0, 0)),
            ],
        ),
        out_shape=[jax.ShapeDtypeStruct((b, 1, ATT_WIDTH), F32), jax.ShapeDtypeStruct((b, SUBLANE, LANE), jnp.int32)],
        compiler_params=_cparams(("parallel",)),
        name="sample_attn_cmp",
    )(slopes, sproj3, kcs, vcs, ovl)


SEL_PER_STEP = 8
BLK_ROWS = SEL_BLOCK * N_KV


def _sattn2_kernel(idx_ref, pt_ref, sl_ref, q_ref, ksn_ref, vsn_ref, kwn_ref, vwn_ref, *refs, past_len, w_buf):
    del pt_ref
    nblk = N_KV * SEL_PER_STEP
    kbs, vbs = refs[:nblk], refs[nblk:2 * nblk]
    kwc_ref, vwc_ref, os_ref, ow_ref, kwo_ref, vwo_ref, qs_ref, m_ref, l_ref, acc_ref = refs[2 * nblk:]
    b = pl.program_id(0)
    jt = pl.program_id(1)
    head_row = lax.broadcasted_iota(jnp.int32, (SUBLANE, 1), 0)

    def slope_col(g):
        slope = jnp.zeros((SUBLANE, 1), F32)
        for r in range(GROUP):
            slope = jnp.where(head_row == r, sl_ref[g * GROUP + r], slope)
        return slope

    @pl.when(jt == 0)
    def _():
        for g in range(N_KV):
            qs_ref[g * SUBLANE:(g + 1) * SUBLANE, :] = _stack_rows(
                [q_ref[:, (g * GROUP + r) * HEAD_DIM:(g * GROUP + r + 1) * HEAD_DIM] for r in range(GROUP)],
                SUBLANE)
        m_ref[...] = jnp.full(m_ref.shape, NEG_INF, F32)
        l_ref[...] = jnp.zeros(l_ref.shape, F32)
        acc_ref[...] = jnp.zeros(acc_ref.shape, F32)

    n_keys = SEL_PER_STEP * SEL_BLOCK
    lane = lax.broadcasted_iota(jnp.int32, (1, n_keys), 1)
    pos_in_blk = lax.broadcasted_iota(jnp.int32, (SEL_BLOCK, 1), 0)
    for g in range(N_KV):
        rows = slice(g * SUBLANE, (g + 1) * SUBLANE)
        kparts, vparts = [], []
        blkv = jnp.zeros((1, n_keys), jnp.int32)
        for jj in range(SEL_PER_STEP):
            blk = idx_ref[b, g, jt * SEL_PER_STEP + jj]
            is_past = (blk * SEL_BLOCK + pos_in_blk) < past_len
            new_k = jnp.broadcast_to(ksn_ref[:, g * HEAD_DIM:(g + 1) * HEAD_DIM], (SEL_BLOCK, HEAD_DIM))
            new_v = jnp.broadcast_to(vsn_ref[:, g * HEAD_DIM:(g + 1) * HEAD_DIM], (SEL_BLOCK, HEAD_DIM))
            kparts.append(jnp.where(is_past, kbs[g * SEL_PER_STEP + jj][pl.ds(g, SEL_BLOCK, stride=N_KV), :], new_k))
            vparts.append(jnp.where(is_past, vbs[g * SEL_PER_STEP + jj][pl.ds(g, SEL_BLOCK, stride=N_KV), :], new_v))
            blkv = jnp.where(lax.shift_right_logical(lane, int(np.log2(SEL_BLOCK))) == jj, blk, blkv)
        kcat = jnp.concatenate(kparts, axis=0).astype(BF16)
        vcat = jnp.concatenate(vparts, axis=0).astype(BF16)
        rel = blkv * SEL_BLOCK + (lane & (SEL_BLOCK - 1)) - past_len
        s = _dot_t(qs_ref[rows, :].astype(BF16), kcat) * SCALE + slope_col(g) * rel.astype(F32)
        s = jnp.where(rel <= 0, s, NEG_INF)
        m_prev = m_ref[rows]
        m_new = jnp.maximum(m_prev, jnp.max(s, axis=-1, keepdims=True))
        m_safe = jnp.where(m_new == NEG_INF, 0.0, m_new)
        p = jnp.exp(s - m_safe)
        alpha = jnp.exp(m_prev - m_safe)
        l_ref[rows] = alpha * l_ref[rows] + jnp.sum(p, axis=-1, keepdims=True)
        acc_ref[rows] = alpha * acc_ref[rows] + jnp.dot(p.astype(BF16), vcat, preferred_element_type=F32)
        m_ref[rows] = m_new

    @pl.when(jt == N_SEL // SEL_PER_STEP - 1)
    def _():
        i = lax.broadcasted_iota(jnp.int32, (1, w_buf + SUBLANE), 1)
        relw = jnp.minimum(i, w_buf) - w_buf
        validw = (relw > -WINDOW) & (i <= w_buf)
        for g in range(N_KV):
            rows = slice(g * SUBLANE, (g + 1) * SUBLANE)
            o = acc_ref[rows] * (1.0 / jnp.maximum(l_ref[rows], 1e-30))
            new_k = jnp.broadcast_to(kwn_ref[:, g * HEAD_DIM:(g + 1) * HEAD_DIM], (SUBLANE, HEAD_DIM))
            new_v = jnp.broadcast_to(vwn_ref[:, g * HEAD_DIM:(g + 1) * HEAD_DIM], (SUBLANE, HEAD_DIM))
            kw = jnp.concatenate([kwc_ref[pl.ds(g, w_buf, stride=N_KV), :], new_k], axis=0).astype(BF16)
            vw = jnp.concatenate([vwc_ref[pl.ds(g, w_buf, stride=N_KV), :], new_v], axis=0).astype(BF16)
            sw = _dot_t(qs_ref[rows, :].astype(BF16), kw) * SCALE + slope_col(g) * relw.astype(F32)
            ow = jnp.dot(_softmax_rows(sw, validw).astype(BF16), vw, preferred_element_type=F32)
            for r in range(GROUP):
                cols = slice((g * GROUP + r) * HEAD_DIM, (g * GROUP + r + 1) * HEAD_DIM)
                os_ref[:, cols] = _take_row(o, r)
                ow_ref[:, cols] = _take_row(ow, r)
        n_rows = w_buf * N_KV
        rowi = lax.broadcasted_iota(jnp.int32, (n_rows, HEAD_DIM), 0)
        for src, new, dst in ((kwc_ref, kwn_ref, kwo_ref), (vwc_ref, vwn_ref, vwo_ref)):
            out = pltpu.roll(src[...], n_rows - N_KV, 0)
            for g in range(N_KV):
                out = jnp.where(rowi == n_rows - N_KV + g, new[:, g * HEAD_DIM:(g + 1) * HEAD_DIM], out)
            dst[...] = out


def _sample_attn2(idx, page_table, slopes, sproj3, pool_k, pool_v, win_k, win_v, past_len):
    b = sproj3.shape[0]
    w_buf = win_k.shape[0] // (b * N_KV)
    n_blk = past_len // SEL_BLOCK
    bpp = PAGE_SIZE // SEL_BLOCK

    def blk_spec(g, jj):
        def index_map(bb, jt, ix, pt, sl):
            blk = jnp.minimum(ix[bb, g, jt * SEL_PER_STEP + jj], n_blk - 1)
            return (pt[bb, blk // bpp] * bpp + blk % bpp, 0)
        return pl.BlockSpec((BLK_ROWS, HEAD_DIM), index_map)

    blk_specs = [blk_spec(g, jj) for g in range(N_KV) for jj in range(SEL_PER_STEP)]
    row = lambda col, w: pl.BlockSpec((None, 1, w), lambda bb, jt, ix, pt, sl: (bb, 0, col // w))
    win_spec = pl.BlockSpec((w_buf * N_KV, HEAD_DIM), lambda bb, jt, ix, pt, sl: (bb, 0))
    o_spec = pl.BlockSpec((None, 1, ATT_WIDTH), lambda bb, jt, ix, pt, sl: (bb, 0, 0))
    n_q = N_KV * SUBLANE
    return pl.pallas_call(
        functools.partial(_sattn2_kernel, past_len=past_len, w_buf=w_buf),
        grid_spec=pltpu.PrefetchScalarGridSpec(
            num_scalar_prefetch=3,
            grid=(b, N_SEL // SEL_PER_STEP),
            in_specs=[row(C_Q, ATT_WIDTH), row(C_KS, KV_WIDTH), row(C_VS, KV_WIDTH), row(C_KW, KV_WIDTH),
                      row(C_VW, KV_WIDTH)] + blk_specs + blk_specs + [win_spec, win_spec],
            out_specs=[o_spec, o_spec, win_spec, win_spec],
            scratch_shapes=[
                pltpu.VMEM((n_q, HEAD_DIM), F32),
                pltpu.VMEM((n_q, 1), F32),
                pltpu.VMEM((n_q, 1), F32),
                pltpu.VMEM((n_q, HEAD_DIM), F32),
            ],
        ),
        out_shape=[jax.ShapeDtypeStruct((b, 1, ATT_WIDTH), F32)] * 2
        + [jax.ShapeDtypeStruct(win_k.shape, F32)] * 2,
        compiler_params=_cparams(("parallel", "arbitrary")),
        name="sample_attn_sel_win",
    )(idx, page_table, slopes, *([sproj3] * 5), *([pool_k] * (N_KV * SEL_PER_STEP)),
      *([pool_v] * (N_KV * SEL_PER_STEP)), win_k, win_v)


def _merge_kernel(oc_ref, os_ref, ow_ref, gl_ref, za_ref, mgc_ref, mga_ref, cd_ref, x_ref, wa_ref, wo_ref, gp_ref,
                  y_ref):
    gates = _sigmoid(gl_ref[...])
    tm = gates.shape[0]
    pieces = []
    for h in range(N_HEADS):
        cols = slice(h * HEAD_DIM, (h + 1) * HEAD_DIM)
        gate = lambda c: jnp.broadcast_to(gates[:, c * N_HEADS + h:c * N_HEADS + h + 1], (tm, HEAD_DIM))
        pieces.append(gate(0) * oc_ref[:, cols] + gate(1) * os_ref[:, cols] + gate(2) * ow_ref[:, cols])
    o = jnp.concatenate(pieces, axis=1)
    attn_d = jnp.dot((o * _silu(za_ref[...])).astype(BF16), wa_ref[...], preferred_element_type=F32)
    mix = _sigmoid(mgc_ref[...]) * cd_ref[...] + _sigmoid(mga_ref[...]) * attn_d
    z = jnp.dot(mix.astype(BF16), wo_ref[...], preferred_element_type=F32)
    ms = jnp.mean(z * z, axis=-1, keepdims=True)
    y_ref[...] = x_ref[...] + z * lax.rsqrt(ms + RMS_EPS) * gp_ref[...]


def _merge(o_c, o_s, o_w, proj, conv_d, x2d, w_ao, w_o, g_post):
    m = x2d.shape[0]
    tm = min(128, m)
    wide = lambda col: pl.BlockSpec((tm, D_MODEL), lambda i: (i, col // D_MODEL))
    plain = pl.BlockSpec((tm, D_MODEL), lambda i: (i, 0))
    return pl.pallas_call(
        _merge_kernel,
        grid=(m // tm,),
        in_specs=[plain, plain, plain,
                  pl.BlockSpec((tm, LANE), lambda i: (i, C_GL // LANE)),
                  wide(C_ZA), wide(C_MGC), wide(C_MGA), plain, plain,
                  _const_spec((ATT_WIDTH, D_MODEL)), _const_spec((D_MODEL, D_MODEL)), _const_spec((1, D_MODEL))],
        out_specs=plain,
        out_shape=jax.ShapeDtypeStruct((m, D_MODEL), F32),
        compiler_params=_cparams(("parallel",)),
        name="merge",
    )(o_c, o_s, o_w, proj, proj, proj, proj, conv_d, x2d, w_ao, w_o, g_post.reshape(1, D_MODEL))


def _alibi_slopes():
    h = jnp.arange(1, N_HEADS + 1, dtype=F32)
    return jnp.exp2(-8.0 * h / N_HEADS)


def _layer_weights(l, w_in, w_dw, w_conv_out, w_attn_out, w_out):
    w_dw_p = jnp.concatenate([w_dw[l], jnp.zeros((HALO - CONV_K, CONV_C), F32)], axis=0)
    return (_proj_weights(w_in[l]), w_dw_p, w_conv_out[l].astype(BF16), w_attn_out[l].astype(BF16),
            w_out[l].astype(BF16))


def _prompt_layer(x, slopes, w_r, w_dw_p, w_co, w_ao, w_o, g_pre, b_dw, ln_g, ln_b, pe_k, w1_k, w2_k, pe_v, w1_v,
                  w2_v, g_post):
    t = x.shape[1]
    x2d = x.reshape(t, D_MODEL)
    proj = _project(x2d, g_pre, *w_r)
    conv_d, tail = _conv_branch(proj, w_dw_p, b_dw, ln_g, ln_b, w_co)
    kv = _kv_rows(proj)
    n_pages = t // PAGE_SIZE
    pt = jnp.arange(n_pages, dtype=jnp.int32).reshape(1, n_pages)
    kcs, vcs = _compress(pt, kv[0], kv[1], pe_k, w1_k, w2_k, pe_v, w1_v, w2_v, pps=min(16, n_pages))
    o_c, member, active = _prompt_attn1(slopes, proj, kcs[0], vcs[0])
    o_s = _prompt_sel(slopes, proj, member, active, _sel_expand_matrix_t(t))
    o_w = _prompt_win(slopes, proj)
    y = _merge(o_c, o_s, o_w, proj, conv_d, x2d, w_ao, w_o, g_post)
    wl = min(WINDOW, t)
    heads = lambda a: a.reshape(1, -1, N_KV, HEAD_DIM)
    states = (heads(kv[0]), heads(kv[1]), heads(kv[2]), heads(kv[3]), heads(kv[4][-wl * N_KV:]),
              heads(kv[5][-wl * N_KV:]), tail[-(CONV_K - 1):].reshape(1, CONV_K - 1, CONV_C))
    return y.reshape(1, t, D_MODEL), states


def _sample_layer(x, l, ck_cmp, cv_cmp, ck_slc, cv_slc, ck_win, cv_win, c_conv, page_table, slopes, w_r, w_dw_p,
                  w_co, w_ao, w_o, g_pre, b_dw, ln_g, ln_b, pe_k, w1_k, w2_k, pe_v, w1_v, w2_v, g_post):
    b = x.shape[0]
    n_pool = ck_cmp.shape[1]
    n_pages = page_table.shape[1]
    past_len = n_pages * PAGE_SIZE
    w_buf = ck_win.shape[2]
    x2d = x.reshape(b, D_MODEL)
    sproj = _project(x2d, g_pre, *w_r)
    sproj3 = sproj.reshape(b, 1, PROJ_W)
    conv_d, s_conv_t = _sample_conv(jnp.transpose(c_conv[l], (1, 0, 2)), sproj, w_dw_p, b_dw, ln_g, ln_b, w_co)
    s_conv = jnp.transpose(s_conv_t, (1, 0, 2))
    pool = lambda a: a[l].reshape(n_pool * PAGE_ROWS, HEAD_DIM)
    win = lambda a: a[l].reshape(b * w_buf * N_KV, HEAD_DIM)
    kcs, vcs = _compress(page_table, pool(ck_cmp), pool(cv_cmp), pe_k, w1_k, w2_k, pe_v, w1_v, w2_v,
                         pps=min(16, n_pages))
    o_c, idx = _sample_attn1(slopes, sproj3, kcs, vcs, past_len)
    o_s, o_w, kwo, vwo = _sample_attn2(idx[:, :N_KV, :N_SEL], page_table, slopes, sproj3, pool(ck_slc), pool(cv_slc),
                                       win(ck_win), win(cv_win), past_len)
    y = _merge(o_c.reshape(b, ATT_WIDTH), o_s.reshape(b, ATT_WIDTH), o_w.reshape(b, ATT_WIDTH), sproj, conv_d, x2d,
               w_ao, w_o, g_post)
    heads = lambda a: a.reshape(b, -1, N_KV, HEAD_DIM)
    new = _kv_rows(sproj)
    states = (heads(new[0]), heads(new[1]), heads(new[2]), heads(new[3]), heads(kwo), heads(vwo), s_conv)
    return y.reshape(b, 1, D_MODEL), states


def kernel(x_prompt, x_sample, cache_k_cmp, cache_v_cmp, cache_k_slc, cache_v_slc, cache_k_win, cache_v_win,
           cache_conv, page_table, g_pre, w_in, w_dw, b_dw, ln_g, ln_b, w_conv_out, pe_k, w1_k, w2_k, pe_v, w1_v,
           w2_v, w_attn_out, w_out, g_post):
    depth = g_pre.shape[0]
    slopes = _alibi_slopes()
    y_prompt, y_sample = x_prompt, x_sample
    p_states, s_states = [], []
    for l in range(depth):
        mats = _layer_weights(l, w_in, w_dw, w_conv_out, w_attn_out, w_out)
        vecs = (g_pre[l], b_dw[l], ln_g[l], ln_b[l], pe_k[l], w1_k[l], w2_k[l], pe_v[l], w1_v[l], w2_v[l], g_post[l])
        y_prompt, sp = _prompt_layer(y_prompt, slopes, *mats, *vecs)
        y_sample, ss = _sample_layer(y_sample, l, cache_k_cmp, cache_v_cmp, cache_k_slc, cache_v_slc, cache_k_win,
                                     cache_v_win, cache_conv, page_table, slopes, *mats, *vecs)
        p_states.append(sp)
        s_states.append(ss)
    p_out = [jnp.stack(a) for a in zip(*p_states)]
    s_out = [jnp.stack(a) for a in zip(*s_states)]
    return (y_prompt, y_sample, *p_out, *s_out)
```

```python
import functools

import numpy as np
import jax
import jax.numpy as jnp
from jax import lax
from jax.experimental import pallas as pl
from jax.experimental.pallas import tpu as pltpu

F32 = jnp.float32
BF16 = jnp.bfloat16

D_MODEL = 2048
CONV_C = D_MODEL
CONV_K = 31
N_HEADS = 16
N_KV = 4
GROUP = N_HEADS // N_KV
HEAD_DIM = 128
ATT_WIDTH = N_HEADS * HEAD_DIM
KV_WIDTH = N_KV * HEAD_DIM
CMP_LEN = 32
CMP_STRIDE = 16
SEL_BLOCK = 64
N_SEL = 16
N_LOCAL = 2
WINDOW = 512
RMS_EPS = 1e-6
LN_EPS = 1e-5
FORCE_BONUS = 1e9
PAGE_SIZE = 128
IN_SIZES = (CONV_C, CONV_C, CONV_C, ATT_WIDTH) + (KV_WIDTH,) * 6 + (ATT_WIDTH, 3 * N_HEADS, D_MODEL, D_MODEL)

LANE = 128
SUBLANE = 8
NEG_INF = float("-inf")
SCALE = HEAD_DIM ** -0.5

C_U, C_UG, C_ZC, C_Q, C_ZA, C_MGC, C_MGA = (i * D_MODEL for i in range(7))
C_KC = 7 * D_MODEL
C_VC, C_KS, C_VS, C_KW, C_VW = (C_KC + i * KV_WIDTH for i in range(1, 6))
C_GL = C_KC + 6 * KV_WIDTH
PROJ_TN = 1024
PROJ_W = C_GL + PROJ_TN
VMEM_LIMIT = 56 * 1024 * 1024


def _cparams(sem):
    return pltpu.CompilerParams(dimension_semantics=sem, vmem_limit_bytes=VMEM_LIMIT)


def _const_spec(shape):
    nd = len(shape)
    return pl.BlockSpec(shape, lambda *a: (0,) * nd, pipeline_mode=pl.Buffered(1))


def _sigmoid(x):
    return jax.nn.sigmoid(x)


def _silu(x):
    return x * jax.nn.sigmoid(x)


def _dot_t(a, b):
    return lax.dot_general(a, b, (((1,), (1,)), ((), ())), preferred_element_type=F32)


def _stack_rows(rows, n):
    w = rows[0].shape[1]
    rid = lax.broadcasted_iota(jnp.int32, (n, 1), 0)
    out = jnp.zeros((n, w), rows[0].dtype)
    for r, v in enumerate(rows):
        out = jnp.where(rid == r, jnp.broadcast_to(v, (n, w)), out)
    return out


def _take_row(x, r):
    rid = lax.broadcasted_iota(jnp.int32, (x.shape[0], 1), 0)
    return jnp.sum(jnp.where(rid == r, x, 0.0), axis=0, keepdims=True)


def _rmsnorm_kernel(x_ref, g_ref, h_ref):
    x = x_ref[...]
    ms = jnp.mean(x * x, axis=-1, keepdims=True)
    h_ref[...] = (x * lax.rsqrt(ms + RMS_EPS) * g_ref[...]).astype(BF16)


def _proj_kernel(a_ix, b_ix, use_b, h_ref, wa_ref, wb_ref, o_ref, w_ref):
    del a_ix, b_ix
    j = pl.program_id(0)

    @pl.when(pl.program_id(1) == 0)
    def _():
        @pl.when(use_b[j] == 0)
        def _():
            w_ref[...] = wa_ref[...].astype(BF16)

        @pl.when(use_b[j] != 0)
        def _():
            w_ref[...] = wb_ref[...].astype(BF16)

    o_ref[...] = _dot_t(h_ref[...], w_ref[...])


def _proj_segments():
    offs = [int(v) for v in np.concatenate([[0], np.cumsum(IN_SIZES)])]
    run = lambda p_col, first, last: (p_col, offs[first], offs[last + 1] - offs[first])
    return (run(C_U, 0, 3), run(C_ZA, 10, 10), run(C_MGC, 12, 13), run(C_KC, 4, 9), run(C_GL, 11, 11))


def _proj_plan():
    n_tiles = PROJ_W // PROJ_TN
    a_ix, b_ix, use_b = (np.zeros(n_tiles, np.int32) for _ in range(3))
    b_parts, b_rows = [], 0
    for p_col, lo, width in _proj_segments():
        hi = lo + width
        n = -(-(hi - lo) // PROJ_TN)
        aligned = lo % PROJ_TN == 0 and (hi - lo) % PROJ_TN == 0
        for k in range(n):
            tile = p_col // PROJ_TN + k
            if aligned:
                a_ix[tile] = lo // PROJ_TN + k
            else:
                use_b[tile], b_ix[tile] = 1, b_rows // PROJ_TN + k
        if not aligned:
            b_parts.append((lo, hi, n * PROJ_TN - (hi - lo)))
            b_rows += n * PROJ_TN
    for tile in range(1, n_tiles):
        if use_b[tile]:
            a_ix[tile] = a_ix[tile - 1]
        else:
            b_ix[tile] = b_ix[tile - 1]
    return a_ix, b_ix, use_b, b_parts


def _proj_weights(w_in_l):
    w_t = jnp.swapaxes(w_in_l, 0, 1)
    parts = []
    for lo, hi, pad in _proj_plan()[3]:
        parts.append(w_t[lo:hi])
        if pad:
            parts.append(jnp.zeros((pad, D_MODEL), w_t.dtype))
    return w_t, jnp.concatenate(parts, axis=0)


def _project(x2d, g_pre, w_t, w_side):
    m = x2d.shape[0]
    tm = min(1024, m)
    h = pl.pallas_call(
        _rmsnorm_kernel,
        grid=(m // tm,),
        in_specs=[pl.BlockSpec((tm, D_MODEL), lambda i: (i, 0)), pl.BlockSpec((1, D_MODEL), lambda i: (0, 0))],
        out_specs=pl.BlockSpec((tm, D_MODEL), lambda i: (i, 0)),
        out_shape=jax.ShapeDtypeStruct((m, D_MODEL), BF16),
        compiler_params=_cparams(("parallel",)),
        name="rmsnorm",
    )(x2d, g_pre.reshape(1, D_MODEL))
    a_ix, b_ix, use_b, _ = _proj_plan()
    return pl.pallas_call(
        _proj_kernel,
        grid_spec=pltpu.PrefetchScalarGridSpec(
            num_scalar_prefetch=3,
            grid=(PROJ_W // PROJ_TN, m // tm),
            in_specs=[
                pl.BlockSpec((tm, D_MODEL), lambda j, i, a, b, u: (i, 0)),
                pl.BlockSpec((PROJ_TN, D_MODEL), lambda j, i, a, b, u: (a[j], 0)),
                pl.BlockSpec((PROJ_TN, D_MODEL), lambda j, i, a, b, u: (b[j], 0), pipeline_mode=pl.Buffered(1)),
            ],
            out_specs=pl.BlockSpec((tm, PROJ_TN), lambda j, i, a, b, u: (i, j)),
            scratch_shapes=[pltpu.VMEM((PROJ_TN, D_MODEL), BF16)],
        ),
        out_shape=jax.ShapeDtypeStruct((m, PROJ_W), F32),
        compiler_params=_cparams(("arbitrary", "arbitrary")),
        name="proj",
    )(jnp.asarray(a_ix), jnp.asarray(b_ix), jnp.asarray(use_b), h, w_t, w_side)


KV_COLS = (C_KC, C_VC, C_KS, C_VS, C_KW, C_VW)


def _kv_rows_kernel(*refs):
    n = len(refs) // 2
    for src, dst in zip(refs[:n], refs[n:]):
        tm = src.shape[0]
        for g in range(N_KV):
            dst[pl.ds(g, tm, stride=N_KV), :] = src[:, g * HEAD_DIM:(g + 1) * HEAD_DIM]


def _kv_rows(proj):
    m = proj.shape[0]
    tm = min(512, m)
    col_spec = lambda c: pl.BlockSpec((tm, KV_WIDTH), lambda i: (i, c // KV_WIDTH))
    return pl.pallas_call(
        _kv_rows_kernel,
        grid=(m // tm,),
        in_specs=[col_spec(c) for c in KV_COLS],
        out_specs=[pl.BlockSpec((tm * N_KV, HEAD_DIM), lambda i: (i, 0))] * len(KV_COLS),
        out_shape=[jax.ShapeDtypeStruct((m * N_KV, HEAD_DIM), F32)] * len(KV_COLS),
        compiler_params=_cparams(("parallel",)),
        name="kv_rows",
    )(*([proj] * len(KV_COLS)))


HALO = 32
CONV_RB = 8


def _ln_gate_out(c, zc, bdw, lng, lnb, wco_ref):
    c = c + bdw
    mean = jnp.mean(c, axis=-1, keepdims=True)
    xc = c - mean
    var = jnp.mean(xc * xc, axis=-1, keepdims=True)
    y = xc * lax.rsqrt(var + LN_EPS) * lng + lnb
    act = _silu(y) * _silu(zc)
    return jnp.dot(act.astype(BF16), wco_ref[...], preferred_element_type=F32)


def _conv_kernel(u_ref, ug_ref, zc_ref, uh_ref, ugh_ref, wdw_ref, bdw_ref, lng_ref, lnb_ref, wco_ref,
                 out_ref, tail_ref, uext_ref, sh_ref, c_ref):
    i = pl.program_id(0)
    tm = u_ref.shape[0]
    glu = u_ref[...] * _sigmoid(ug_ref[...])
    halo = uh_ref[...] * _sigmoid(ugh_ref[...])
    not_first = (jnp.zeros((HALO, 1), jnp.int32) + i) > 0
    uext_ref[0:HALO, :] = jnp.where(not_first, halo, 0.0)
    uext_ref[HALO:HALO + tm, :] = glu
    tail_ref[...] = glu[tm - HALO:, :]
    n_sh = tm + HALO - SUBLANE
    for p in range(1, SUBLANE):
        sh_ref[p - 1] = uext_ref[p:p + n_sh, :]

    off = HALO - (CONV_K - 1)

    def body(rc, carry):
        r0 = pl.multiple_of(rc * CONV_RB, CONV_RB)
        acc = jnp.zeros((CONV_RB, CONV_C), F32)
        for k in range(CONV_K):
            p, base = (off + k) % SUBLANE, (off + k) // SUBLANE * SUBLANE
            if p == 0:
                win = uext_ref[pl.ds(r0 + base, CONV_RB), :]
            else:
                win = sh_ref[p - 1, pl.ds(r0 + base, CONV_RB), :]
            acc = acc + win * wdw_ref[k]
        c_ref[pl.ds(r0, CONV_RB), :] = acc
        return carry

    lax.fori_loop(0, tm // CONV_RB, body, 0)
    out_ref[...] = _ln_gate_out(c_ref[...], zc_ref[...], bdw_ref[...], lng_ref[...], lnb_ref[...], wco_ref)


def _conv_branch(proj, w_dw_p, b_dw, ln_g, ln_b, w_co):
    t = proj.shape[0]
    tm = min(256, t)
    hb = tm // HALO
    row = lambda a: a.reshape(1, CONV_C)
    return pl.pallas_call(
        _conv_kernel,
        grid=(t // tm,),
        in_specs=[
            pl.BlockSpec((tm, CONV_C), lambda i: (i, C_U // CONV_C)),
            pl.BlockSpec((tm, CONV_C), lambda i: (i, C_UG // CONV_C)),
            pl.BlockSpec((tm, CONV_C), lambda i: (i, C_ZC // CONV_C)),
            pl.BlockSpec((HALO, CONV_C), lambda i: (jnp.maximum(i * hb - 1, 0), C_U // CONV_C)),
            pl.BlockSpec((HALO, CONV_C), lambda i: (jnp.maximum(i * hb - 1, 0), C_UG // CONV_C)),
            _const_spec((CONV_K, CONV_RB, CONV_C)),
            _const_spec((1, CONV_C)),
            _const_spec((1, CONV_C)),
            _const_spec((1, CONV_C)),
            _const_spec((CONV_C, D_MODEL)),
        ],
        out_specs=[
            pl.BlockSpec((tm, D_MODEL), lambda i: (i, 0)),
            pl.BlockSpec((HALO, CONV_C), lambda i: (0, 0)),
        ],
        out_shape=[jax.ShapeDtypeStruct((t, D_MODEL), F32), jax.ShapeDtypeStruct((HALO, CONV_C), F32)],
        scratch_shapes=[pltpu.VMEM((tm + HALO, CONV_C), F32),
                        pltpu.VMEM((SUBLANE - 1, tm + HALO - SUBLANE, CONV_C), F32),
                        pltpu.VMEM((tm, CONV_C), F32)],
        compiler_params=_cparams(("arbitrary",)),
        name="conv_branch",
    )(proj, proj, proj, proj, proj,
      jnp.broadcast_to(w_dw_p[:CONV_K, None, :], (CONV_K, CONV_RB, CONV_C)),
      row(b_dw), row(ln_g), row(ln_b), w_co)


def _sconv_kernel(cc_ref, u_ref, ug_ref, zc_ref, wdw_ref, bdw_ref, lng_ref, lnb_ref, wco_ref,
                  out_ref, state_ref):
    nh = CONV_K - 1
    glu = u_ref[...] * _sigmoid(ug_ref[...])
    c = glu * wdw_ref[nh:nh + 1, :]
    for k in range(nh):
        c = c + cc_ref[k] * wdw_ref[k:k + 1, :]
    out_ref[...] = _ln_gate_out(c, zc_ref[...], bdw_ref[...], lng_ref[...], lnb_ref[...], wco_ref)
    for k in range(nh - 1):
        state_ref[k] = cc_ref[k + 1]
    state_ref[nh - 1] = glu


def _sample_conv(cc, sproj, w_dw_p, b_dw, ln_g, ln_b, w_co):
    nh, b = cc.shape[0], cc.shape[1]
    row = lambda a: a.reshape(1, CONV_C)
    return pl.pallas_call(
        _sconv_kernel,
        grid=(1,),
        in_specs=[
            pl.BlockSpec((nh, b, CONV_C), lambda i: (0, 0, 0)),
            pl.BlockSpec((b, CONV_C), lambda i: (0, C_U // CONV_C)),
            pl.BlockSpec((b, CONV_C), lambda i: (0, C_UG // CONV_C)),
            pl.BlockSpec((b, CONV_C), lambda i: (0, C_ZC // CONV_C)),
            pl.BlockSpec((HALO, CONV_C), lambda i: (0, 0)),
            pl.BlockSpec((1, CONV_C), lambda i: (0, 0)),
            pl.BlockSpec((1, CONV_C), lambda i: (0, 0)),
            pl.BlockSpec((1, CONV_C), lambda i: (0, 0)),
            pl.BlockSpec((CONV_C, D_MODEL), lambda i: (0, 0)),
        ],
        out_specs=[
            pl.BlockSpec((b, D_MODEL), lambda i: (0, 0)),
            pl.BlockSpec((nh, b, CONV_C), lambda i: (0, 0, 0)),
        ],
        out_shape=[jax.ShapeDtypeStruct((b, D_MODEL), F32), jax.ShapeDtypeStruct((nh, b, CONV_C), F32)],
        compiler_params=_cparams(("arbitrary",)),
        name="sample_conv",
    )(cc, sproj, sproj, sproj, w_dw_p, row(b_dw), row(ln_g), row(ln_b), w_co)


SUB_PER_PAGE = PAGE_SIZE // CMP_STRIDE
PAGE_ROWS = PAGE_SIZE * N_KV
R_N = CMP_LEN // CMP_STRIDE
STEP_PERM = tuple(range(0, CMP_STRIDE, 2)) + tuple(range(1, CMP_STRIDE, 2))
HALF_SUB = SUBLANE // 2


def _compress_kernel(pt_ref, *refs, pps):
    del pt_ref
    pages = (refs[:pps], refs[pps:2 * pps])
    w01s = refs[2 * pps:2 * pps + 2]
    pes = refs[2 * pps + 2:2 * pps + 4]
    w2s = refs[2 * pps + 4:2 * pps + 6]
    outs = refs[2 * pps + 6:2 * pps + 8]
    carries = refs[2 * pps + 8:2 * pps + 10]
    xs_refs = refs[2 * pps + 10:2 * pps + 12]
    rows = pps * SUB_PER_PAGE * N_KV
    n_c = CMP_STRIDE // 2
    grp = n_c * SUBLANE

    @pl.when(pl.program_id(1) == 0)
    def _():
        carries[0][...] = jnp.zeros_like(carries[0])
        carries[1][...] = jnp.zeros_like(carries[1])

    top3 = lax.broadcasted_iota(jnp.int32, (n_c, SUBLANE, HEAD_DIM), 1) < HALF_SUB
    top = lax.broadcasted_iota(jnp.int32, (SUBLANE, HEAD_DIM), 0) < HALF_SUB
    for which in range(2):
        xs_ref = xs_refs[which]
        for i, pg in enumerate(pages[which]):
            los, his = [], []
            for p in range(SUB_PER_PAGE // 2):
                a = pg[(2 * p) * grp:(2 * p + 1) * grp, :].reshape(n_c, SUBLANE, HEAD_DIM)
                b = pg[(2 * p + 1) * grp:(2 * p + 2) * grp, :].reshape(n_c, SUBLANE, HEAD_DIM)
                los.append(jnp.where(top3, a, pltpu.roll(b, HALF_SUB, 1)))
                his.append(jnp.where(top3, pltpu.roll(a, HALF_SUB, 1), b))
            for pp in range(len(los) // 2):
                r0 = (i * (SUB_PER_PAGE // 2) + 2 * pp) * SUBLANE
                for c in range(n_c):
                    for half, parts in ((0, los), (1, his)):
                        col = (half * n_c + c) * HEAD_DIM
                        xs_ref[r0:r0 + 2 * SUBLANE, col:col + HEAD_DIM] = jnp.concatenate(
                            [parts[2 * pp][c], parts[2 * pp + 1][c]], axis=0).astype(BF16)
        w01 = w01s[which][...]
        a = jnp.dot(xs_ref[...], w01, preferred_element_type=F32)
        pe_o = jnp.dot(pes[which][...].astype(BF16), w01, preferred_element_type=F32)
        hid0 = pe_o[0:1, :HEAD_DIM] + pe_o[1:2, HEAD_DIM:]
        carry = carries[which]
        a0, a1 = a[:, :HEAD_DIM], a[:, HEAD_DIM:]
        a0s = pltpu.roll(a0, N_KV, 0)
        first = jnp.where(top, carry[...], a0s[0:SUBLANE])
        a0s = jnp.concatenate([first, a0s[SUBLANE:]], axis=0)
        carry[...] = pltpu.roll(a0[rows - SUBLANE:rows, :], N_KV, 0)
        hid = hid0 + a0s + a1
        outs[which][...] = jnp.dot(_silu(hid).astype(BF16), w2s[which][...], preferred_element_type=F32)


def _w01(w1):
    w = w1.reshape(R_N, CMP_STRIDE, HEAD_DIM, HEAD_DIM)[:, STEP_PERM, :, :]
    w = w.reshape(R_N, CMP_STRIDE * HEAD_DIM, HEAD_DIM)
    return jnp.concatenate([w[r] for r in range(R_N)], axis=1).astype(BF16)


def _pe2(pe):
    p = pe.reshape(R_N, CMP_STRIDE, HEAD_DIM)[:, STEP_PERM, :].reshape(R_N, CMP_STRIDE * HEAD_DIM)
    return jnp.concatenate([p, jnp.zeros((SUBLANE - R_N, CMP_STRIDE * HEAD_DIM), F32)], axis=0)


def _compress(page_table, cache_k, cache_v, pe_k, w1_k, w2_k, pe_v, w1_v, w2_v, pps):
    b, n_pages = page_table.shape
    rows = pps * SUB_PER_PAGE * N_KV
    n_sub = n_pages * SUB_PER_PAGE

    def page_spec(i):
        return pl.BlockSpec((PAGE_ROWS, HEAD_DIM), lambda bb, c, pt: (pt[bb, c * pps + i], 0))

    cspec = lambda shape: pl.BlockSpec(shape, lambda bb, c, pt: (0,) * len(shape))
    in_specs = [page_spec(i) for i in range(pps)] * 2 + [
        cspec((CMP_STRIDE * HEAD_DIM, R_N * HEAD_DIM))] * 2 + [
        cspec((SUBLANE, CMP_STRIDE * HEAD_DIM))] * 2 + [cspec((HEAD_DIM, HEAD_DIM))] * 2
    out_spec = pl.BlockSpec((None, rows, HEAD_DIM), lambda bb, c, pt: (bb, c, 0))
    return pl.pallas_call(
        functools.partial(_compress_kernel, pps=pps),
        grid_spec=pltpu.PrefetchScalarGridSpec(
            num_scalar_prefetch=1,
            grid=(b, n_pages // pps),
            in_specs=in_specs,
            out_specs=[out_spec, out_spec],
            scratch_shapes=[pltpu.VMEM((SUBLANE, HEAD_DIM), F32)] * 2
            + [pltpu.VMEM((rows, CMP_STRIDE * HEAD_DIM), BF16)] * 2,
        ),
        out_shape=[jax.ShapeDtypeStruct((b, n_sub * N_KV, HEAD_DIM), F32)] * 2,
        compiler_params=_cparams(("parallel", "arbitrary")),
        name="compress",
    )(page_table, *([cache_k] * pps), *([cache_v] * pps), _w01(w1_k), _w01(w1_v), _pe2(pe_k), _pe2(pe_v),
      w2_k.astype(BF16), w2_v.astype(BF16))


def _ovl_shifted(n_sub, n_lanes):
    i = (np.arange(n_sub)[:, None] - 1) * CMP_STRIDE
    j = np.arange(n_lanes)[None, :] * SEL_BLOCK
    ov = np.clip(np.minimum(i + CMP_LEN, j + SEL_BLOCK) - np.maximum(i, j), 0, None).astype(np.float32) / CMP_LEN
    ov[0, :] = 0.0
    return jnp.asarray(ov, dtype=BF16)


def _softmax_rows(s, valid):
    s = jnp.where(valid, s, NEG_INF)
    mx = jnp.max(s, axis=-1, keepdims=True)
    mx = jnp.where(mx == NEG_INF, 0.0, mx)
    p = jnp.exp(s - mx)
    den = jnp.maximum(jnp.sum(p, axis=-1, keepdims=True), 1e-30)
    return p * (1.0 / den)


def _select_blocks(imp, qpos, n_sel_blocks):
    r, l = imp.shape
    lane = lax.broadcasted_iota(jnp.int32, (r, l), 1)
    lanef = lane.astype(F32)
    cur = lax.shift_right_logical(qpos, int(np.log2(SEL_BLOCK)))
    valid = lane * SEL_BLOCK <= qpos
    forced = (lane == 0) | ((lane <= cur) & (lane > cur - N_LOCAL))
    score = jnp.where(valid, imp + jnp.where(forced, FORCE_BONUS, 0.0), -FORCE_BONUS)
    score = jnp.where(lane < n_sel_blocks, score, -3e38)
    member = jnp.zeros((r, l), F32)
    picks = []
    for _ in range(N_SEL):
        mx = jnp.max(score, axis=-1, keepdims=True)
        idx = jnp.min(jnp.where(score == mx, lanef, float(l)), axis=-1, keepdims=True)
        hit = lanef == idx
        member = jnp.where(hit, 1.0, member)
        score = jnp.where(hit, NEG_INF, score)
        picks.append(idx)
    return picks, member


def _attn1_kernel(sl_ref, q_ref, kc_ref, vc_ref, ovl_ref, tile_ref, oc_ref, mem_ref, act_ref, *, tq, n_sub,
                  n_sel_blocks):
    qi = pl.program_id(0)
    qpos = qi * tq + lax.broadcasted_iota(jnp.int32, (tq, 1), 0)
    m = lax.broadcasted_iota(jnp.int32, (1, n_sub), 1)
    c_end = (m - 1) * CMP_STRIDE + (CMP_LEN - 1)
    c_ctr = ((m - 1) * CMP_STRIDE).astype(F32) + (CMP_LEN - 1) / 2
    valid = (m >= 1) & (c_end <= qpos)
    dist = jnp.abs(qpos.astype(F32) - c_ctr)
    any_blk = []
    for g in range(N_KV):
        qg = jnp.concatenate([q_ref[:, (g * GROUP + r) * HEAD_DIM:(g * GROUP + r + 1) * HEAD_DIM]
                              for r in range(GROUP)], axis=0).astype(BF16)
        kg = kc_ref[pl.ds(g, n_sub, stride=N_KV), :].astype(BF16)
        s = _dot_t(qg, kg)
        ps = []
        for r in range(GROUP):
            sr = s[r * tq:(r + 1) * tq] * SCALE - sl_ref[g * GROUP + r] * dist
            ps.append(_softmax_rows(sr, valid))
        psum = ps[0]
        for r in range(1, GROUP):
            psum = psum + ps[r]
        pcat = jnp.concatenate(ps, axis=0).astype(BF16)
        oc = jnp.dot(pcat, vc_ref[pl.ds(g, n_sub, stride=N_KV), :].astype(BF16), preferred_element_type=F32)
        for r in range(GROUP):
            oc_ref[:, (g * GROUP + r) * HEAD_DIM:(g * GROUP + r + 1) * HEAD_DIM] = oc[r * tq:(r + 1) * tq]
        imp = jnp.dot(psum.astype(BF16), ovl_ref[...], preferred_element_type=F32)
        _, member = _select_blocks(imp, qpos, n_sel_blocks)
        mem_ref[g] = member.astype(BF16)
        any_blk.append(jnp.max(member, axis=0, keepdims=True))
    hits = jnp.dot(_stack_rows(any_blk, SUBLANE).astype(BF16), tile_ref[...], preferred_element_type=F32)
    act_ref[...] = (hits > 0.5).astype(jnp.int32)


def _prompt_attn1(slopes, proj, kcs, vcs):
    t = proj.shape[0]
    n_sub = kcs.shape[0] // N_KV
    tq = min(ATTN1_TQ, t)
    n_sel_blocks = -(-t // SEL_BLOCK)
    blocks_per_tile = min(SEL_TK, t) // SEL_BLOCK
    assert n_sel_blocks <= LANE and n_sel_blocks <= blocks_per_tile * N_KT_LANES
    ovl = _ovl_shifted(n_sub, LANE)
    blk = np.arange(LANE)[:, None]
    tile_of_block = jnp.asarray((blk // blocks_per_tile == np.arange(LANE)[None, :]) & (blk != 0), dtype=BF16)
    o_c, member, act = pl.pallas_call(
        functools.partial(_attn1_kernel, tq=tq, n_sub=n_sub, n_sel_blocks=n_sel_blocks),
        grid_spec=pltpu.PrefetchScalarGridSpec(
            num_scalar_prefetch=1,
            grid=(t // tq,),
            in_specs=[
                pl.BlockSpec((tq, ATT_WIDTH), lambda i, sl: (i, C_Q // ATT_WIDTH)),
                pl.BlockSpec((n_sub * N_KV, HEAD_DIM), lambda i, sl: (0, 0)),
                pl.BlockSpec((n_sub * N_KV, HEAD_DIM), lambda i, sl: (0, 0)),
                pl.BlockSpec((n_sub, LANE), lambda i, sl: (0, 0)),
                pl.BlockSpec((LANE, LANE), lambda i, sl: (0, 0)),
            ],
            out_specs=[
                pl.BlockSpec((tq, ATT_WIDTH), lambda i, sl: (i, 0)),
                pl.BlockSpec((N_KV, tq, LANE), lambda i, sl: (0, i, 0)),
                pl.BlockSpec((SUBLANE, LANE), lambda i, sl: (i, 0)),
            ],
        ),
        out_shape=[jax.ShapeDtypeStruct((t, ATT_WIDTH), F32), jax.ShapeDtypeStruct((N_KV, t, LANE), BF16),
                   jax.ShapeDtypeStruct((t // tq * SUBLANE, LANE), jnp.int32)],
        compiler_params=_cparams(("parallel",)),
        name="prompt_attn_cmp",
    )(slopes, proj, kcs, vcs, ovl, tile_of_block)
    return o_c, member, act[:, :N_KT_LANES].reshape(-1)


LOG2E = float(np.log2(np.e))


SEL_TQ = 256
SEL_TK = 1024
ATTN1_TQ = 256
N_KT_LANES = LANE // (SEL_TK // SEL_BLOCK)


def _stack_heads(q_ref, tq):
    return jnp.concatenate([(q_ref[:, r * HEAD_DIM:(r + 1) * HEAD_DIM] * (SCALE * LOG2E)).astype(BF16)
                            for r in range(GROUP)], axis=0)


def _sel_kernel(sl_ref, qt_ref, kt_ref, fl_ref, act_ref, q_ref, k_ref, v_ref, mem_ref, et_ref, o_ref,
                qs_ref, m_ref, l_ref, acc_ref, *, tq, tk):
    g = pl.program_id(0)
    step = pl.program_id(1)
    qi = qt_ref[step]
    kt = kt_ref[step]
    flags = fl_ref[step]

    @pl.when((flags & 1) != 0)
    def _():
        qs_ref[...] = _stack_heads(q_ref, tq)
        m_ref[...] = jnp.full(m_ref.shape, NEG_INF, F32)
        l_ref[...] = jnp.zeros(l_ref.shape, F32)
        acc_ref[...] = jnp.zeros(acc_ref.shape, F32)

    sub = tq // ATTN1_TQ
    active = act_ref[((qi * sub) * SUBLANE + g) * N_KT_LANES + kt]
    for h in range(1, sub):
        active = active | act_ref[((qi * sub + h) * SUBLANE + g) * N_KT_LANES + kt]

    def attend(n):
        st = _dot_t(k_ref[0:n, :].astype(BF16), qs_ref[...])
        kpos = kt * tk + lax.broadcasted_iota(jnp.int32, (n, 1), 0)
        qpos = qi * tq + lax.broadcasted_iota(jnp.int32, (1, tq), 1)
        mask = (_dot_t(et_ref[0:n, :], mem_ref[...]) > 0.5) & (kpos <= qpos)
        koff = (kpos - qi * tq).astype(F32)
        vt = v_ref[0:n, :].T.astype(BF16)
        for r in range(GROUP):
            cols = slice(r * tq, (r + 1) * tq)
            sr = jnp.where(mask, st[:, cols] + (sl_ref[g * GROUP + r] * LOG2E) * koff, NEG_INF)
            m_prev = m_ref[:, cols]
            m_new = jnp.maximum(m_prev, jnp.max(sr, axis=0, keepdims=True))
            m_safe = jnp.where(m_new == NEG_INF, 0.0, m_new)
            p = jnp.exp2(sr - m_safe)
            alpha = jnp.exp2(m_prev - m_safe)
            l_ref[:, cols] = alpha * l_ref[:, cols] + jnp.sum(p, axis=0, keepdims=True)
            acc_ref[:, cols] = alpha * acc_ref[:, cols] + jnp.dot(vt, p.astype(BF16), preferred_element_type=F32)
            m_ref[:, cols] = m_new

    @pl.when(active != 0)
    def _():
        attend(tk)

    @pl.when((active == 0) & (kt == 0))
    def _():
        attend(LANE)

    @pl.when((flags & 2) != 0)
    def _():
        for r in range(GROUP):
            cols = slice(r * tq, (r + 1) * tq)
            o = acc_ref[:, cols] * (1.0 / jnp.maximum(l_ref[:, cols], 1e-30))
            o_ref[:, r * HEAD_DIM:(r + 1) * HEAD_DIM] = o.T


def _sel_steps(t, tq, tk):
    qt, kt, fl = [], [], []
    for qi in range(t // tq):
        hi = (qi * tq + tq - 1) // tk
        for k in range(hi + 1):
            qt.append(qi)
            kt.append(k)
            fl.append((1 if k == 0 else 0) | (2 if k == hi else 0))
    as_i32 = lambda a: jnp.asarray(np.asarray(a, np.int32))
    return as_i32(qt), as_i32(kt), as_i32(fl)


def _prompt_sel(slopes, proj, member, active, emat_t):
    t = proj.shape[0]
    tq, tk = min(SEL_TQ, t), min(SEL_TK, t)
    qt, kt, fl = _sel_steps(t, tq, tk)
    imap = lambda f: (lambda g, s, sl, qt, kt, fl, act: f(g, qt[s], kt[s]))
    return pl.pallas_call(
        functools.partial(_sel_kernel, tq=tq, tk=tk),
        grid_spec=pltpu.PrefetchScalarGridSpec(
            num_scalar_prefetch=5,
            grid=(N_KV, int(qt.shape[0])),
            in_specs=[
                pl.BlockSpec((tq, GROUP * HEAD_DIM), imap(lambda g, qi, ki: (qi, C_Q // (GROUP * HEAD_DIM) + g))),
                pl.BlockSpec((tk, HEAD_DIM), imap(lambda g, qi, ki: (ki, C_KS // HEAD_DIM + g))),
                pl.BlockSpec((tk, HEAD_DIM), imap(lambda g, qi, ki: (ki, C_VS // HEAD_DIM + g))),
                pl.BlockSpec((None, tq, LANE), imap(lambda g, qi, ki: (g, qi, 0))),
                pl.BlockSpec((tk, LANE), imap(lambda g, qi, ki: (ki, 0))),
            ],
            out_specs=pl.BlockSpec((tq, GROUP * HEAD_DIM), imap(lambda g, qi, ki: (qi, g))),
            scratch_shapes=[
                pltpu.VMEM((GROUP * tq, HEAD_DIM), BF16),
                pltpu.VMEM((1, GROUP * tq), F32),
                pltpu.VMEM((1, GROUP * tq), F32),
                pltpu.VMEM((HEAD_DIM, GROUP * tq), F32),
            ],
        ),
        out_shape=jax.ShapeDtypeStruct((t, ATT_WIDTH), F32),
        compiler_params=_cparams(("parallel", "arbitrary")),
        name="prompt_attn_sel",
    )(slopes, qt, kt, fl, active, proj, proj, proj, member, emat_t)


def _win_kernel(sl_ref, q_ref, *refs, tq, n_kt):
    k_refs, v_refs, o_ref = refs[:n_kt], refs[n_kt:2 * n_kt], refs[2 * n_kt]
    g = pl.program_id(0)
    qi = pl.program_id(1)
    qs = _stack_heads(q_ref, tq)
    k = jnp.concatenate([kr[...] for kr in k_refs], axis=0).astype(BF16)
    vt = jnp.concatenate([vr[...] for vr in v_refs], axis=0).T.astype(BF16)
    st = _dot_t(k, qs)
    kpos = (qi - (n_kt - 1)) * tq + lax.broadcasted_iota(jnp.int32, (n_kt * tq, 1), 0)
    qpos = qi * tq + lax.broadcasted_iota(jnp.int32, (1, tq), 1)
    rel = kpos - qpos
    mask = (rel <= 0) & (rel > -WINDOW) & (kpos >= 0)
    koff = (kpos - qi * tq).astype(F32)
    for r in range(GROUP):
        sr = jnp.where(mask, st[:, r * tq:(r + 1) * tq] + (sl_ref[g * GROUP + r] * LOG2E) * koff, NEG_INF)
        mx = jnp.max(sr, axis=0, keepdims=True)
        p = jnp.exp2(sr - jnp.where(mx == NEG_INF, 0.0, mx))
        den = jnp.maximum(jnp.sum(p, axis=0, keepdims=True), 1e-30)
        o = jnp.dot(vt, p.astype(BF16), preferred_element_type=F32) * (1.0 / den)
        o_ref[:, r * HEAD_DIM:(r + 1) * HEAD_DIM] = o.T


def _prompt_win(slopes, proj):
    t = proj.shape[0]
    tq = min(SEL_TQ, t)
    n_kt = -(-(WINDOW - 1) // tq) + 1
    kv_spec = lambda col, j: pl.BlockSpec(
        (tq, HEAD_DIM), lambda g, qi, sl: (jnp.maximum(qi - (n_kt - 1) + j, 0), col // HEAD_DIM + g))
    return pl.pallas_call(
        functools.partial(_win_kernel, tq=tq, n_kt=n_kt),
        grid_spec=pltpu.PrefetchScalarGridSpec(
            num_scalar_prefetch=1,
            grid=(N_KV, t // tq),
            in_specs=[pl.BlockSpec((tq, GROUP * HEAD_DIM), lambda g, qi, sl: (qi, C_Q // (GROUP * HEAD_DIM) + g))]
            + [kv_spec(C_KW, j) for j in range(n_kt)] + [kv_spec(C_VW, j) for j in range(n_kt)],
            out_specs=pl.BlockSpec((tq, GROUP * HEAD_DIM), lambda g, qi, sl: (qi, g)),
        ),
        out_shape=jax.ShapeDtypeStruct((t, ATT_WIDTH), F32),
        compiler_params=_cparams(("parallel", "parallel")),
        name="prompt_attn_win",
    )(slopes, *([proj] * (1 + 2 * n_kt)))


def _sel_expand_matrix_t(t):
    e = (np.arange(t)[:, None] // SEL_BLOCK) == np.arange(LANE)[None, :]
    return jnp.asarray(e, dtype=BF16)


def _sattn1_kernel(sl_ref, q_ref, kc_ref, vc_ref, ovl_ref, oc_ref, idx_ref, *, n_sub, qpos, n_sel_blocks, n_lanes):
    m = lax.broadcasted_iota(jnp.int32, (1, n_sub), 1)
    c_end = (m - 1) * CMP_STRIDE + (CMP_LEN - 1)
    c_ctr = ((m - 1) * CMP_STRIDE).astype(F32) + (CMP_LEN - 1) / 2
    valid = (m >= 1) & (c_end <= qpos)
    dist = jnp.abs(float(qpos) - c_ctr)
    qposv = jnp.full((SUBLANE, 1), qpos, jnp.int32)
    out_lane = lax.broadcasted_iota(jnp.int32, (SUBLANE, LANE), 1)
    idx_acc = jnp.zeros((SUBLANE, LANE), F32)
    head_row = lax.broadcasted_iota(jnp.int32, (SUBLANE, 1), 0)
    psums = []
    for g in range(N_KV):
        qg = _stack_rows([q_ref[:, (g * GROUP + r) * HEAD_DIM:(g * GROUP + r + 1) * HEAD_DIM]
                          for r in range(GROUP)], SUBLANE).astype(BF16)
        s = _dot_t(qg, kc_ref[pl.ds(g, n_sub, stride=N_KV), :].astype(BF16))
        slope = jnp.zeros((SUBLANE, 1), F32)
        for r in range(GROUP):
            slope = jnp.where(head_row == r, sl_ref[g * GROUP + r], slope)
        p = _softmax_rows(s * SCALE - slope * dist, valid)
        p = jnp.where(head_row < GROUP, p, 0.0)
        oc = jnp.dot(p.astype(BF16), vc_ref[pl.ds(g, n_sub, stride=N_KV), :].astype(BF16),
                     preferred_element_type=F32)
        for r in range(GROUP):
            oc_ref[:, (g * GROUP + r) * HEAD_DIM:(g * GROUP + r + 1) * HEAD_DIM] = _take_row(oc, r)
        psums.append(jnp.sum(p, axis=0, keepdims=True))
    imp = jnp.dot(_stack_rows(psums, SUBLANE).astype(BF16), ovl_ref[...], preferred_element_type=F32)
    picks, _ = _select_blocks(imp, qposv, n_sel_blocks)
    for j, pk in enumerate(picks):
        idx_acc = jnp.where(out_lane == j, pk, idx_acc)
    idx_ref[...] = idx_acc.astype(jnp.int32)


def _sample_attn1(slopes, sproj3, kcs, vcs, past_len):
    b = sproj3.shape[0]
    n_sub = kcs.shape[1] // N_KV
    n_sel_blocks = -(-(past_len + 1) // SEL_BLOCK)
    n_lanes = -(-n_sel_blocks // LANE) * LANE
    ovl = _ovl_shifted(n_sub, n_lanes)
    return pl.pallas_call(
        functools.partial(_sattn1_kernel, n_sub=n_sub, qpos=past_len, n_sel_blocks=n_sel_blocks, n_lanes=n_lanes),
        grid_spec=pltpu.PrefetchScalarGridSpec(
            num_scalar_prefetch=1,
            grid=(b,),
            in_specs=[
                pl.BlockSpec((None, 1, ATT_WIDTH), lambda i, sl: (i, 0, C_Q // ATT_WIDTH)),
                pl.BlockSpec((None, n_sub * N_KV, HEAD_DIM), lambda i, sl: (i, 0, 0)),
                pl.BlockSpec((None, n_sub * N_KV, HEAD_DIM), lambda i, sl: (i, 0, 0)),
                pl.BlockSpec((n_sub, n_lanes), lambda i, sl: (0, 0)),
            ],
            out_specs=[
                pl.BlockSpec((None, 1, ATT_WIDTH), lambda i, sl: (i, 0, 0)),
                pl.BlockSpec((None, SUBLANE, LANE), lambda i, sl: (i, 0, 0)),
            ],
        ),
        out_shape=[jax.ShapeDtypeStruct((b, 1, ATT_WIDTH), F32), jax.ShapeDtypeStruct((b, SUBLANE, LANE), jnp.int32)],
        compiler_params=_cparams(("parallel",)),
        name="sample_attn_cmp",
    )(slopes, sproj3, kcs, vcs, ovl)


SEL_PER_STEP = 8
BLK_ROWS = SEL_BLOCK * N_KV


def _sattn2_kernel(idx_ref, pt_ref, sl_ref, q_ref, ksn_ref, vsn_ref, kwn_ref, vwn_ref, *refs, past_len, w_buf):
    del pt_ref
    nblk = N_KV * SEL_PER_STEP
    kbs, vbs = refs[:nblk], refs[nblk:2 * nblk]
    kwc_ref, vwc_ref, os_ref, ow_ref, kwo_ref, vwo_ref, qs_ref, m_ref, l_ref, acc_ref = refs[2 * nblk:]
    b = pl.program_id(0)
    jt = pl.program_id(1)
    head_row = lax.broadcasted_iota(jnp.int32, (SUBLANE, 1), 0)

    def slope_col(g):
        slope = jnp.zeros((SUBLANE, 1), F32)
        for r in range(GROUP):
            slope = jnp.where(head_row == r, sl_ref[g * GROUP + r], slope)
        return slope

    @pl.when(jt == 0)
    def _():
        for g in range(N_KV):
            qs_ref[g * SUBLANE:(g + 1) * SUBLANE, :] = _stack_rows(
                [q_ref[:, (g * GROUP + r) * HEAD_DIM:(g * GROUP + r + 1) * HEAD_DIM] for r in range(GROUP)],
                SUBLANE)
        m_ref[...] = jnp.full(m_ref.shape, NEG_INF, F32)
        l_ref[...] = jnp.zeros(l_ref.shape, F32)
        acc_ref[...] = jnp.zeros(acc_ref.shape, F32)

    n_keys = SEL_PER_STEP * SEL_BLOCK
    lane = lax.broadcasted_iota(jnp.int32, (1, n_keys), 1)
    pos_in_blk = lax.broadcasted_iota(jnp.int32, (SEL_BLOCK, 1), 0)
    for g in range(N_KV):
        rows = slice(g * SUBLANE, (g + 1) * SUBLANE)
        kparts, vparts = [], []
        blkv = jnp.zeros((1, n_keys), jnp.int32)
        for jj in range(SEL_PER_STEP):
            blk = idx_ref[b, g, jt * SEL_PER_STEP + jj]
            is_past = (blk * SEL_BLOCK + pos_in_blk) < past_len
            new_k = jnp.broadcast_to(ksn_ref[:, g * HEAD_DIM:(g + 1) * HEAD_DIM], (SEL_BLOCK, HEAD_DIM))
            new_v = jnp.broadcast_to(vsn_ref[:, g * HEAD_DIM:(g + 1) * HEAD_DIM], (SEL_BLOCK, HEAD_DIM))
            kparts.append(jnp.where(is_past, kbs[g * SEL_PER_STEP + jj][pl.ds(g, SEL_BLOCK, stride=N_KV), :], new_k))
            vparts.append(jnp.where(is_past, vbs[g * SEL_PER_STEP + jj][pl.ds(g, SEL_BLOCK, stride=N_KV), :], new_v))
            blkv = jnp.where(lax.shift_right_logical(lane, int(np.log2(SEL_BLOCK))) == jj, blk, blkv)
        kcat = jnp.concatenate(kparts, axis=0).astype(BF16)
        vcat = jnp.concatenate(vparts, axis=0).astype(BF16)
        rel = blkv * SEL_BLOCK + (lane & (SEL_BLOCK - 1)) - past_len
        s = _dot_t(qs_ref[rows, :].astype(BF16), kcat) * SCALE + slope_col(g) * rel.astype(F32)
        s = jnp.where(rel <= 0, s, NEG_INF)
        m_prev = m_ref[rows]
        m_new = jnp.maximum(m_prev, jnp.max(s, axis=-1, keepdims=True))
        m_safe = jnp.where(m_new == NEG_INF, 0.0, m_new)
        p = jnp.exp(s - m_safe)
        alpha = jnp.exp(m_prev - m_safe)
        l_ref[rows] = alpha * l_ref[rows] + jnp.sum(p, axis=-1, keepdims=True)
        acc_ref[rows] = alpha * acc_ref[rows] + jnp.dot(p.astype(BF16), vcat, preferred_element_type=F32)
        m_ref[rows] = m_new

    @pl.when(jt == N_SEL // SEL_PER_STEP - 1)
    def _():
        i = lax.broadcasted_iota(jnp.int32, (1, w_buf + SUBLANE), 1)
        relw = jnp.minimum(i, w_buf) - w_buf
        validw = (relw > -WINDOW) & (i <= w_buf)
        for g in range(N_KV):
            rows = slice(g * SUBLANE, (g + 1) * SUBLANE)
            o = acc_ref[rows] * (1.0 / jnp.maximum(l_ref[rows], 1e-30))
            new_k = jnp.broadcast_to(kwn_ref[:, g * HEAD_DIM:(g + 1) * HEAD_DIM], (SUBLANE, HEAD_DIM))
            new_v = jnp.broadcast_to(vwn_ref[:, g * HEAD_DIM:(g + 1) * HEAD_DIM], (SUBLANE, HEAD_DIM))
            kw = jnp.concatenate([kwc_ref[pl.ds(g, w_buf, stride=N_KV), :], new_k], axis=0).astype(BF16)
            vw = jnp.concatenate([vwc_ref[pl.ds(g, w_buf, stride=N_KV), :], new_v], axis=0).astype(BF16)
            sw = _dot_t(qs_ref[rows, :].astype(BF16), kw) * SCALE + slope_col(g) * relw.astype(F32)
            ow = jnp.dot(_softmax_rows(sw, validw).astype(BF16), vw, preferred_element_type=F32)
            for r in range(GROUP):
                cols = slice((g * GROUP + r) * HEAD_DIM, (g * GROUP + r + 1) * HEAD_DIM)
                os_ref[:, cols] = _take_row(o, r)
                ow_ref[:, cols] = _take_row(ow, r)
        n_rows = w_buf * N_KV
        rowi = lax.broadcasted_iota(jnp.int32, (n_rows, HEAD_DIM), 0)
        for src, new, dst in ((kwc_ref, kwn_ref, kwo_ref), (vwc_ref, vwn_ref, vwo_ref)):
            out = pltpu.roll(src[...], n_rows - N_KV, 0)
            for g in range(N_KV):
                out = jnp.where(rowi == n_rows - N_KV + g, new[:, g * HEAD_DIM:(g + 1) * HEAD_DIM], out)
            dst[...] = out


def _sample_attn2(idx, page_table, slopes, sproj3, pool_k, pool_v, win_k, win_v, past_len):
    b = sproj3.shape[0]
    w_buf = win_k.shape[0] // (b * N_KV)
    n_blk = past_len // SEL_BLOCK
    bpp = PAGE_SIZE // SEL_BLOCK

    def blk_spec(g, jj):
        def index_map(bb, jt, ix, pt, sl):
            blk = jnp.minimum(ix[bb, g, jt * SEL_PER_STEP + jj], n_blk - 1)
            return (pt[bb, blk // bpp] * bpp + blk % bpp, 0)
        return pl.BlockSpec((BLK_ROWS, HEAD_DIM), index_map)

    blk_specs = [blk_spec(g, jj) for g in range(N_KV) for jj in range(SEL_PER_STEP)]
    row = lambda col, w: pl.BlockSpec((None, 1, w), lambda bb, jt, ix, pt, sl: (bb, 0, col // w))
    win_spec = pl.BlockSpec((w_buf * N_KV, HEAD_DIM), lambda bb, jt, ix, pt, sl: (bb, 0))
    o_spec = pl.BlockSpec((None, 1, ATT_WIDTH), lambda bb, jt, ix, pt, sl: (bb, 0, 0))
    n_q = N_KV * SUBLANE
    return pl.pallas_call(
        functools.partial(_sattn2_kernel, past_len=past_len, w_buf=w_buf),
        grid_spec=pltpu.PrefetchScalarGridSpec(
            num_scalar_prefetch=3,
            grid=(b, N_SEL // SEL_PER_STEP),
            in_specs=[row(C_Q, ATT_WIDTH), row(C_KS, KV_WIDTH), row(C_VS, KV_WIDTH), row(C_KW, KV_WIDTH),
                      row(C_VW, KV_WIDTH)] + blk_specs + blk_specs + [win_spec, win_spec],
            out_specs=[o_spec, o_spec, win_spec, win_spec],
            scratch_shapes=[
                pltpu.VMEM((n_q, HEAD_DIM), F32),
                pltpu.VMEM((n_q, 1), F32),
                pltpu.VMEM((n_q, 1), F32),
                pltpu.VMEM((n_q, HEAD_DIM), F32),
            ],
        ),
        out_shape=[jax.ShapeDtypeStruct((b, 1, ATT_WIDTH), F32)] * 2
        + [jax.ShapeDtypeStruct(win_k.shape, F32)] * 2,
        compiler_params=_cparams(("parallel", "arbitrary")),
        name="sample_attn_sel_win",
    )(idx, page_table, slopes, *([sproj3] * 5), *([pool_k] * (N_KV * SEL_PER_STEP)),
      *([pool_v] * (N_KV * SEL_PER_STEP)), win_k, win_v)


def _merge_kernel(oc_ref, os_ref, ow_ref, gl_ref, za_ref, mgc_ref, mga_ref, cd_ref, x_ref, wa_ref, wo_ref, gp_ref,
                  y_ref):
    gates = _sigmoid(gl_ref[...])
    tm = gates.shape[0]
    pieces = []
    for h in range(N_HEADS):
        cols = slice(h * HEAD_DIM, (h + 1) * HEAD_DIM)
        gate = lambda c: jnp.broadcast_to(gates[:, c * N_HEADS + h:c * N_HEADS + h + 1], (tm, HEAD_DIM))
        pieces.append(gate(0) * oc_ref[:, cols] + gate(1) * os_ref[:, cols] + gate(2) * ow_ref[:, cols])
    o = jnp.concatenate(pieces, axis=1)
    attn_d = jnp.dot((o * _silu(za_ref[...])).astype(BF16), wa_ref[...], preferred_element_type=F32)
    mix = _sigmoid(mgc_ref[...]) * cd_ref[...] + _sigmoid(mga_ref[...]) * attn_d
    z = jnp.dot(mix.astype(BF16), wo_ref[...], preferred_element_type=F32)
    ms = jnp.mean(z * z, axis=-1, keepdims=True)
    y_ref[...] = x_ref[...] + z * lax.rsqrt(ms + RMS_EPS) * gp_ref[...]


def _merge(o_c, o_s, o_w, proj, conv_d, x2d, w_ao, w_o, g_post):
    m = x2d.shape[0]
    tm = min(128, m)
    wide = lambda col: pl.BlockSpec((tm, D_MODEL), lambda i: (i, col // D_MODEL))
    plain = pl.BlockSpec((tm, D_MODEL), lambda i: (i, 0))
    return pl.pallas_call(
        _merge_kernel,
        grid=(m // tm,),
        in_specs=[plain, plain, plain,
                  pl.BlockSpec((tm, LANE), lambda i: (i, C_GL // LANE)),
                  wide(C_ZA), wide(C_MGC), wide(C_MGA), plain, plain,
                  _const_spec((ATT_WIDTH, D_MODEL)), _const_spec((D_MODEL, D_MODEL)), _const_spec((1, D_MODEL))],
        out_specs=plain,
        out_shape=jax.ShapeDtypeStruct((m, D_MODEL), F32),
        compiler_params=_cparams(("parallel",)),
        name="merge",
    )(o_c, o_s, o_w, proj, proj, proj, proj, conv_d, x2d, w_ao, w_o, g_post.reshape(1, D_MODEL))


def _alibi_slopes():
    h = jnp.arange(1, N_HEADS + 1, dtype=F32)
    return jnp.exp2(-8.0 * h / N_HEADS)


def _layer_weights(l, w_in, w_dw, w_conv_out, w_attn_out, w_out):
    w_dw_p = jnp.concatenate([w_dw[l], jnp.zeros((HALO - CONV_K, CONV_C), F32)], axis=0)
    return (_proj_weights(w_in[l]), w_dw_p, w_conv_out[l].astype(BF16), w_attn_out[l].astype(BF16),
            w_out[l].astype(BF16))


def _prompt_layer(x, slopes, w_r, w_dw_p, w_co, w_ao, w_o, g_pre, b_dw, ln_g, ln_b, pe_k, w1_k, w2_k, pe_v, w1_v,
                  w2_v, g_post):
    t = x.shape[1]
    x2d = x.reshape(t, D_MODEL)
    proj = _project(x2d, g_pre, *w_r)
    conv_d, tail = _conv_branch(proj, w_dw_p, b_dw, ln_g, ln_b, w_co)
    kv = _kv_rows(proj)
    n_pages = t // PAGE_SIZE
    pt = jnp.arange(n_pages, dtype=jnp.int32).reshape(1, n_pages)
    kcs, vcs = _compress(pt, kv[0], kv[1], pe_k, w1_k, w2_k, pe_v, w1_v, w2_v, pps=min(16, n_pages))
    o_c, member, active = _prompt_attn1(slopes, proj, kcs[0], vcs[0])
    o_s = _prompt_sel(slopes, proj, member, active, _sel_expand_matrix_t(t))
    o_w = _prompt_win(slopes, proj)
    y = _merge(o_c, o_s, o_w, proj, conv_d, x2d, w_ao, w_o, g_post)
    wl = min(WINDOW, t)
    heads = lambda a: a.reshape(1, -1, N_KV, HEAD_DIM)
    states = (heads(kv[0]), heads(kv[1]), heads(kv[2]), heads(kv[3]), heads(kv[4][-wl * N_KV:]),
              heads(kv[5][-wl * N_KV:]), tail[-(CONV_K - 1):].reshape(1, CONV_K - 1, CONV_C))
    return y.reshape(1, t, D_MODEL), states


def _sample_layer(x, l, ck_cmp, cv_cmp, ck_slc, cv_slc, ck_win, cv_win, c_conv, page_table, slopes, w_r, w_dw_p,
                  w_co, w_ao, w_o, g_pre, b_dw, ln_g, ln_b, pe_k, w1_k, w2_k, pe_v, w1_v, w2_v, g_post):
    b = x.shape[0]
    n_pool = ck_cmp.shape[1]
    n_pages = page_table.shape[1]
    past_len = n_pages * PAGE_SIZE
    w_buf = ck_win.shape[2]
    x2d = x.reshape(b, D_MODEL)
    sproj = _project(x2d, g_pre, *w_r)
    sproj3 = sproj.reshape(b, 1, PROJ_W)
    conv_d, s_conv_t = _sample_conv(jnp.transpose(c_conv[l], (1, 0, 2)), sproj, w_dw_p, b_dw, ln_g, ln_b, w_co)
    s_conv = jnp.transpose(s_conv_t, (1, 0, 2))
    pool = lambda a: a[l].reshape(n_pool * PAGE_ROWS, HEAD_DIM)
    win = lambda a: a[l].reshape(b * w_buf * N_KV, HEAD_DIM)
    kcs, vcs = _compress(page_table, pool(ck_cmp), pool(cv_cmp), pe_k, w1_k, w2_k, pe_v, w1_v, w2_v,
                         pps=min(32, n_pages))
    o_c, idx = _sample_attn1(slopes, sproj3, kcs, vcs, past_len)
    o_s, o_w, kwo, vwo = _sample_attn2(idx[:, :N_KV, :N_SEL], page_table, slopes, sproj3, pool(ck_slc), pool(cv_slc),
                                       win(ck_win), win(cv_win), past_len)
    y = _merge(o_c.reshape(b, ATT_WIDTH), o_s.reshape(b, ATT_WIDTH), o_w.reshape(b, ATT_WIDTH), sproj, conv_d, x2d,
               w_ao, w_o, g_post)
    heads = lambda a: a.reshape(b, -1, N_KV, HEAD_DIM)
    new = _kv_rows(sproj)
    states = (heads(new[0]), heads(new[1]), heads(new[2]), heads(new[3]), heads(kwo), heads(vwo), s_conv)
    return y.reshape(b, 1, D_MODEL), states


def kernel(x_prompt, x_sample, cache_k_cmp, cache_v_cmp, cache_k_slc, cache_v_slc, cache_k_win, cache_v_win,
           cache_conv, page_table, g_pre, w_in, w_dw, b_dw, ln_g, ln_b, w_conv_out, pe_k, w1_k, w2_k, pe_v, w1_v,
           w2_v, w_attn_out, w_out, g_post):
    depth = g_pre.shape[0]
    slopes = _alibi_slopes()
    y_prompt, y_sample = x_prompt, x_sample
    p_states, s_states = [], []
    for l in range(depth):
        mats = _layer_weights(l, w_in, w_dw, w_conv_out, w_attn_out, w_out)
        vecs = (g_pre[l], b_dw[l], ln_g[l], ln_b[l], pe_k[l], w1_k[l], w2_k[l], pe_v[l], w1_v[l], w2_v[l], g_post[l])
        y_prompt, sp = _prompt_layer(y_prompt, slopes, *mats, *vecs)
        y_sample, ss = _sample_layer(y_sample, l, cache_k_cmp, cache_v_cmp, cache_k_slc, cache_v_slc, cache_k_win,
                                     cache_v_win, cache_conv, page_table, slopes, *mats, *vecs)
        p_states.append(sp)
        s_states.append(ss)
    p_out = [jnp.stack(a) for a in zip(*p_states)]
    s_out = [jnp.stack(a) for a in zip(*s_states)]
    return (y_prompt, y_sample, *p_out, *s_out)
```

```python
import functools

import numpy as np
import jax
import jax.numpy as jnp
from jax import lax
from jax.experimental import pallas as pl
from jax.experimental.pallas import tpu as pltpu

F32 = jnp.float32
BF16 = jnp.bfloat16

D_MODEL = 2048
CONV_C = D_MODEL
CONV_K = 31
N_HEADS = 16
N_KV = 4
GROUP = N_HEADS // N_KV
HEAD_DIM = 128
ATT_WIDTH = N_HEADS * HEAD_DIM
KV_WIDTH = N_KV * HEAD_DIM
CMP_LEN = 32
CMP_STRIDE = 16
SEL_BLOCK = 64
N_SEL = 16
N_LOCAL = 2
WINDOW = 512
RMS_EPS = 1e-6
LN_EPS = 1e-5
FORCE_BONUS = 1e9
PAGE_SIZE = 128
IN_SIZES = (CONV_C, CONV_C, CONV_C, ATT_WIDTH) + (KV_WIDTH,) * 6 + (ATT_WIDTH, 3 * N_HEADS, D_MODEL, D_MODEL)

LANE = 128
SUBLANE = 8
NEG_INF = float("-inf")
SCALE = HEAD_DIM ** -0.5

C_U, C_UG, C_ZC, C_Q, C_ZA, C_MGC, C_MGA = (i * D_MODEL for i in range(7))
C_KC = 7 * D_MODEL
C_VC, C_KS, C_VS, C_KW, C_VW = (C_KC + i * KV_WIDTH for i in range(1, 6))
C_GL = C_KC + 6 * KV_WIDTH
PROJ_TN = 1024
PROJ_W = C_GL + PROJ_TN
VMEM_LIMIT = 56 * 1024 * 1024


def _cparams(sem):
    return pltpu.CompilerParams(dimension_semantics=sem, vmem_limit_bytes=VMEM_LIMIT)


def _const_spec(shape):
    nd = len(shape)
    return pl.BlockSpec(shape, lambda *a: (0,) * nd, pipeline_mode=pl.Buffered(1))


def _sigmoid(x):
    return jax.nn.sigmoid(x)


def _silu(x):
    return x * jax.nn.sigmoid(x)


def _dot_t(a, b):
    return lax.dot_general(a, b, (((1,), (1,)), ((), ())), preferred_element_type=F32)


def _stack_rows(rows, n):
    w = rows[0].shape[1]
    rid = lax.broadcasted_iota(jnp.int32, (n, 1), 0)
    out = jnp.zeros((n, w), rows[0].dtype)
    for r, v in enumerate(rows):
        out = jnp.where(rid == r, jnp.broadcast_to(v, (n, w)), out)
    return out


def _take_row(x, r):
    rid = lax.broadcasted_iota(jnp.int32, (x.shape[0], 1), 0)
    return jnp.sum(jnp.where(rid == r, x, 0.0), axis=0, keepdims=True)


def _rmsnorm_kernel(x_ref, g_ref, h_ref):
    x = x_ref[...]
    ms = jnp.mean(x * x, axis=-1, keepdims=True)
    h_ref[...] = (x * lax.rsqrt(ms + RMS_EPS) * g_ref[...]).astype(BF16)


def _proj_kernel(a_ix, b_ix, use_b, h_ref, wa_ref, wb_ref, o_ref, w_ref):
    del a_ix, b_ix
    j = pl.program_id(0)

    @pl.when(pl.program_id(1) == 0)
    def _():
        @pl.when(use_b[j] == 0)
        def _():
            w_ref[...] = wa_ref[...].astype(BF16)

        @pl.when(use_b[j] != 0)
        def _():
            w_ref[...] = wb_ref[...].astype(BF16)

    o_ref[...] = _dot_t(h_ref[...], w_ref[...])


def _proj_segments():
    offs = [int(v) for v in np.concatenate([[0], np.cumsum(IN_SIZES)])]
    run = lambda p_col, first, last: (p_col, offs[first], offs[last + 1] - offs[first])
    return (run(C_U, 0, 3), run(C_ZA, 10, 10), run(C_MGC, 12, 13), run(C_KC, 4, 9), run(C_GL, 11, 11))


def _proj_plan():
    n_tiles = PROJ_W // PROJ_TN
    a_ix, b_ix, use_b = (np.zeros(n_tiles, np.int32) for _ in range(3))
    b_parts, b_rows = [], 0
    for p_col, lo, width in _proj_segments():
        hi = lo + width
        n = -(-(hi - lo) // PROJ_TN)
        aligned = lo % PROJ_TN == 0 and (hi - lo) % PROJ_TN == 0
        for k in range(n):
            tile = p_col // PROJ_TN + k
            if aligned:
                a_ix[tile] = lo // PROJ_TN + k
            else:
                use_b[tile], b_ix[tile] = 1, b_rows // PROJ_TN + k
        if not aligned:
            b_parts.append((lo, hi, n * PROJ_TN - (hi - lo)))
            b_rows += n * PROJ_TN
    for tile in range(1, n_tiles):
        if use_b[tile]:
            a_ix[tile] = a_ix[tile - 1]
        else:
            b_ix[tile] = b_ix[tile - 1]
    return a_ix, b_ix, use_b, b_parts


def _proj_weights(w_in_l):
    w_t = jnp.swapaxes(w_in_l, 0, 1)
    parts = []
    for lo, hi, pad in _proj_plan()[3]:
        parts.append(w_t[lo:hi])
        if pad:
            parts.append(jnp.zeros((pad, D_MODEL), w_t.dtype))
    return w_t, jnp.concatenate(parts, axis=0)


def _project(x2d, g_pre, w_t, w_side):
    m = x2d.shape[0]
    tm = min(1024, m)
    h = pl.pallas_call(
        _rmsnorm_kernel,
        grid=(m // tm,),
        in_specs=[pl.BlockSpec((tm, D_MODEL), lambda i: (i, 0)), pl.BlockSpec((1, D_MODEL), lambda i: (0, 0))],
        out_specs=pl.BlockSpec((tm, D_MODEL), lambda i: (i, 0)),
        out_shape=jax.ShapeDtypeStruct((m, D_MODEL), BF16),
        compiler_params=_cparams(("parallel",)),
        name="rmsnorm",
    )(x2d, g_pre.reshape(1, D_MODEL))
    a_ix, b_ix, use_b, _ = _proj_plan()
    return pl.pallas_call(
        _proj_kernel,
        grid_spec=pltpu.PrefetchScalarGridSpec(
            num_scalar_prefetch=3,
            grid=(PROJ_W // PROJ_TN, m // tm),
            in_specs=[
                pl.BlockSpec((tm, D_MODEL), lambda j, i, a, b, u: (i, 0)),
                pl.BlockSpec((PROJ_TN, D_MODEL), lambda j, i, a, b, u: (a[j], 0)),
                pl.BlockSpec((PROJ_TN, D_MODEL), lambda j, i, a, b, u: (b[j], 0), pipeline_mode=pl.Buffered(1)),
            ],
            out_specs=pl.BlockSpec((tm, PROJ_TN), lambda j, i, a, b, u: (i, j)),
            scratch_shapes=[pltpu.VMEM((PROJ_TN, D_MODEL), BF16)],
        ),
        out_shape=jax.ShapeDtypeStruct((m, PROJ_W), F32),
        compiler_params=_cparams(("arbitrary", "arbitrary")),
        name="proj",
    )(jnp.asarray(a_ix), jnp.asarray(b_ix), jnp.asarray(use_b), h, w_t, w_side)


KV_COLS = (C_KC, C_VC, C_KS, C_VS, C_KW, C_VW)


def _kv_rows_kernel(*refs):
    n = len(refs) // 2
    for src, dst in zip(refs[:n], refs[n:]):
        tm = src.shape[0]
        for g in range(N_KV):
            dst[pl.ds(g, tm, stride=N_KV), :] = src[:, g * HEAD_DIM:(g + 1) * HEAD_DIM]


def _kv_rows(proj):
    m = proj.shape[0]
    tm = min(512, m)
    col_spec = lambda c: pl.BlockSpec((tm, KV_WIDTH), lambda i: (i, c // KV_WIDTH))
    return pl.pallas_call(
        _kv_rows_kernel,
        grid=(m // tm,),
        in_specs=[col_spec(c) for c in KV_COLS],
        out_specs=[pl.BlockSpec((tm * N_KV, HEAD_DIM), lambda i: (i, 0))] * len(KV_COLS),
        out_shape=[jax.ShapeDtypeStruct((m * N_KV, HEAD_DIM), F32)] * len(KV_COLS),
        compiler_params=_cparams(("parallel",)),
        name="kv_rows",
    )(*([proj] * len(KV_COLS)))


HALO = 32
CONV_RB = 16


def _ln_gate_out(c, zc, bdw, lng, lnb, wco_ref):
    c = c + bdw
    mean = jnp.mean(c, axis=-1, keepdims=True)
    xc = c - mean
    var = jnp.mean(xc * xc, axis=-1, keepdims=True)
    y = xc * lax.rsqrt(var + LN_EPS) * lng + lnb
    act = _silu(y) * _silu(zc)
    return jnp.dot(act.astype(BF16), wco_ref[...], preferred_element_type=F32)


def _conv_kernel(u_ref, ug_ref, zc_ref, uh_ref, ugh_ref, wdw_ref, bdw_ref, lng_ref, lnb_ref, wco_ref,
                 out_ref, tail_ref, uext_ref, sh_ref, c_ref):
    i = pl.program_id(0)
    tm = u_ref.shape[0]
    glu = u_ref[...] * _sigmoid(ug_ref[...])
    halo = uh_ref[...] * _sigmoid(ugh_ref[...])
    not_first = (jnp.zeros((HALO, 1), jnp.int32) + i) > 0
    uext_ref[0:HALO, :] = jnp.where(not_first, halo, 0.0)
    uext_ref[HALO:HALO + tm, :] = glu
    tail_ref[...] = glu[tm - HALO:, :]
    n_sh = tm + HALO - SUBLANE
    for p in range(1, SUBLANE):
        sh_ref[p - 1] = uext_ref[p:p + n_sh, :]

    off = HALO - (CONV_K - 1)

    def body(rc, carry):
        r0 = pl.multiple_of(rc * CONV_RB, CONV_RB)
        acc = jnp.zeros((CONV_RB, CONV_C), F32)
        for k in range(CONV_K):
            p, base = (off + k) % SUBLANE, (off + k) // SUBLANE * SUBLANE
            if p == 0:
                win = uext_ref[pl.ds(r0 + base, CONV_RB), :]
            else:
                win = sh_ref[p - 1, pl.ds(r0 + base, CONV_RB), :]
            acc = acc + win * wdw_ref[k]
        c_ref[pl.ds(r0, CONV_RB), :] = acc
        return carry

    lax.fori_loop(0, tm // CONV_RB, body, 0)
    out_ref[...] = _ln_gate_out(c_ref[...], zc_ref[...], bdw_ref[...], lng_ref[...], lnb_ref[...], wco_ref)


def _conv_branch(proj, w_dw_p, b_dw, ln_g, ln_b, w_co):
    t = proj.shape[0]
    tm = min(256, t)
    hb = tm // HALO
    row = lambda a: a.reshape(1, CONV_C)
    return pl.pallas_call(
        _conv_kernel,
        grid=(t // tm,),
        in_specs=[
            pl.BlockSpec((tm, CONV_C), lambda i: (i, C_U // CONV_C)),
            pl.BlockSpec((tm, CONV_C), lambda i: (i, C_UG // CONV_C)),
            pl.BlockSpec((tm, CONV_C), lambda i: (i, C_ZC // CONV_C)),
            pl.BlockSpec((HALO, CONV_C), lambda i: (jnp.maximum(i * hb - 1, 0), C_U // CONV_C)),
            pl.BlockSpec((HALO, CONV_C), lambda i: (jnp.maximum(i * hb - 1, 0), C_UG // CONV_C)),
            _const_spec((CONV_K, CONV_RB, CONV_C)),
            _const_spec((1, CONV_C)),
            _const_spec((1, CONV_C)),
            _const_spec((1, CONV_C)),
            _const_spec((CONV_C, D_MODEL)),
        ],
        out_specs=[
            pl.BlockSpec((tm, D_MODEL), lambda i: (i, 0)),
            pl.BlockSpec((HALO, CONV_C), lambda i: (0, 0)),
        ],
        out_shape=[jax.ShapeDtypeStruct((t, D_MODEL), F32), jax.ShapeDtypeStruct((HALO, CONV_C), F32)],
        scratch_shapes=[pltpu.VMEM((tm + HALO, CONV_C), F32),
                        pltpu.VMEM((SUBLANE - 1, tm + HALO - SUBLANE, CONV_C), F32),
                        pltpu.VMEM((tm, CONV_C), F32)],
        compiler_params=_cparams(("arbitrary",)),
        name="conv_branch",
    )(proj, proj, proj, proj, proj,
      jnp.broadcast_to(w_dw_p[:CONV_K, None, :], (CONV_K, CONV_RB, CONV_C)),
      row(b_dw), row(ln_g), row(ln_b), w_co)


def _sconv_kernel(cc_ref, u_ref, ug_ref, zc_ref, wdw_ref, bdw_ref, lng_ref, lnb_ref, wco_ref,
                  out_ref, state_ref):
    nh = CONV_K - 1
    glu = u_ref[...] * _sigmoid(ug_ref[...])
    c = glu * wdw_ref[nh:nh + 1, :]
    for k in range(nh):
        c = c + cc_ref[k] * wdw_ref[k:k + 1, :]
    out_ref[...] = _ln_gate_out(c, zc_ref[...], bdw_ref[...], lng_ref[...], lnb_ref[...], wco_ref)
    for k in range(nh - 1):
        state_ref[k] = cc_ref[k + 1]
    state_ref[nh - 1] = glu


def _sample_conv(cc, sproj, w_dw_p, b_dw, ln_g, ln_b, w_co):
    nh, b = cc.shape[0], cc.shape[1]
    row = lambda a: a.reshape(1, CONV_C)
    return pl.pallas_call(
        _sconv_kernel,
        grid=(1,),
        in_specs=[
            pl.BlockSpec((nh, b, CONV_C), lambda i: (0, 0, 0)),
            pl.BlockSpec((b, CONV_C), lambda i: (0, C_U // CONV_C)),
            pl.BlockSpec((b, CONV_C), lambda i: (0, C_UG // CONV_C)),
            pl.BlockSpec((b, CONV_C), lambda i: (0, C_ZC // CONV_C)),
            pl.BlockSpec((HALO, CONV_C), lambda i: (0, 0)),
            pl.BlockSpec((1, CONV_C), lambda i: (0, 0)),
            pl.BlockSpec((1, CONV_C), lambda i: (0, 0)),
            pl.BlockSpec((1, CONV_C), lambda i: (0, 0)),
            pl.BlockSpec((CONV_C, D_MODEL), lambda i: (0, 0)),
        ],
        out_specs=[
            pl.BlockSpec((b, D_MODEL), lambda i: (0, 0)),
            pl.BlockSpec((nh, b, CONV_C), lambda i: (0, 0, 0)),
        ],
        out_shape=[jax.ShapeDtypeStruct((b, D_MODEL), F32), jax.ShapeDtypeStruct((nh, b, CONV_C), F32)],
        compiler_params=_cparams(("arbitrary",)),
        name="sample_conv",
    )(cc, sproj, sproj, sproj, w_dw_p, row(b_dw), row(ln_g), row(ln_b), w_co)


SUB_PER_PAGE = PAGE_SIZE // CMP_STRIDE
PAGE_ROWS = PAGE_SIZE * N_KV
R_N = CMP_LEN // CMP_STRIDE
STEP_PERM = tuple(range(0, CMP_STRIDE, 2)) + tuple(range(1, CMP_STRIDE, 2))
HALF_SUB = SUBLANE // 2


def _compress_kernel(pt_ref, *refs, pps):
    del pt_ref
    pages = (refs[:pps], refs[pps:2 * pps])
    w01s = refs[2 * pps:2 * pps + 2]
    pes = refs[2 * pps + 2:2 * pps + 4]
    w2s = refs[2 * pps + 4:2 * pps + 6]
    outs = refs[2 * pps + 6:2 * pps + 8]
    carries = refs[2 * pps + 8:2 * pps + 10]
    xs_refs = refs[2 * pps + 10:2 * pps + 12]
    rows = pps * SUB_PER_PAGE * N_KV
    n_c = CMP_STRIDE // 2
    grp = n_c * SUBLANE

    @pl.when(pl.program_id(1) == 0)
    def _():
        carries[0][...] = jnp.zeros_like(carries[0])
        carries[1][...] = jnp.zeros_like(carries[1])

    top3 = lax.broadcasted_iota(jnp.int32, (n_c, SUBLANE, HEAD_DIM), 1) < HALF_SUB
    top = lax.broadcasted_iota(jnp.int32, (SUBLANE, HEAD_DIM), 0) < HALF_SUB
    for which in range(2):
        xs_ref = xs_refs[which]
        for i, pg in enumerate(pages[which]):
            los, his = [], []
            for p in range(SUB_PER_PAGE // 2):
                a = pg[(2 * p) * grp:(2 * p + 1) * grp, :].reshape(n_c, SUBLANE, HEAD_DIM)
                b = pg[(2 * p + 1) * grp:(2 * p + 2) * grp, :].reshape(n_c, SUBLANE, HEAD_DIM)
                los.append(jnp.where(top3, a, pltpu.roll(b, HALF_SUB, 1)))
                his.append(jnp.where(top3, pltpu.roll(a, HALF_SUB, 1), b))
            for pp in range(len(los) // 2):
                r0 = (i * (SUB_PER_PAGE // 2) + 2 * pp) * SUBLANE
                for c in range(n_c):
                    for half, parts in ((0, los), (1, his)):
                        col = (half * n_c + c) * HEAD_DIM
                        xs_ref[r0:r0 + 2 * SUBLANE, col:col + HEAD_DIM] = jnp.concatenate(
                            [parts[2 * pp][c], parts[2 * pp + 1][c]], axis=0).astype(BF16)
        w01 = w01s[which][...]
        a = jnp.dot(xs_ref[...], w01, preferred_element_type=F32)
        pe_o = jnp.dot(pes[which][...].astype(BF16), w01, preferred_element_type=F32)
        hid0 = pe_o[0:1, :HEAD_DIM] + pe_o[1:2, HEAD_DIM:]
        carry = carries[which]
        a0, a1 = a[:, :HEAD_DIM], a[:, HEAD_DIM:]
        a0s = pltpu.roll(a0, N_KV, 0)
        first = jnp.where(top, carry[...], a0s[0:SUBLANE])
        a0s = jnp.concatenate([first, a0s[SUBLANE:]], axis=0)
        carry[...] = pltpu.roll(a0[rows - SUBLANE:rows, :], N_KV, 0)
        hid = hid0 + a0s + a1
        outs[which][...] = jnp.dot(_silu(hid).astype(BF16), w2s[which][...], preferred_element_type=F32)


def _w01(w1):
    w = w1.reshape(R_N, CMP_STRIDE, HEAD_DIM, HEAD_DIM)[:, STEP_PERM, :, :]
    w = w.reshape(R_N, CMP_STRIDE * HEAD_DIM, HEAD_DIM)
    return jnp.concatenate([w[r] for r in range(R_N)], axis=1).astype(BF16)


def _pe2(pe):
    p = pe.reshape(R_N, CMP_STRIDE, HEAD_DIM)[:, STEP_PERM, :].reshape(R_N, CMP_STRIDE * HEAD_DIM)
    return jnp.concatenate([p, jnp.zeros((SUBLANE - R_N, CMP_STRIDE * HEAD_DIM), F32)], axis=0)


def _compress(page_table, cache_k, cache_v, pe_k, w1_k, w2_k, pe_v, w1_v, w2_v, pps):
    b, n_pages = page_table.shape
    rows = pps * SUB_PER_PAGE * N_KV
    n_sub = n_pages * SUB_PER_PAGE

    def page_spec(i):
        return pl.BlockSpec((PAGE_ROWS, HEAD_DIM), lambda bb, c, pt: (pt[bb, c * pps + i], 0))

    cspec = lambda shape: pl.BlockSpec(shape, lambda bb, c, pt: (0,) * len(shape))
    in_specs = [page_spec(i) for i in range(pps)] * 2 + [
        cspec((CMP_STRIDE * HEAD_DIM, R_N * HEAD_DIM))] * 2 + [
        cspec((SUBLANE, CMP_STRIDE * HEAD_DIM))] * 2 + [cspec((HEAD_DIM, HEAD_DIM))] * 2
    out_spec = pl.BlockSpec((None, rows, HEAD_DIM), lambda bb, c, pt: (bb, c, 0))
    return pl.pallas_call(
        functools.partial(_compress_kernel, pps=pps),
        grid_spec=pltpu.PrefetchScalarGridSpec(
            num_scalar_prefetch=1,
            grid=(b, n_pages // pps),
            in_specs=in_specs,
            out_specs=[out_spec, out_spec],
            scratch_shapes=[pltpu.VMEM((SUBLANE, HEAD_DIM), F32)] * 2
            + [pltpu.VMEM((rows, CMP_STRIDE * HEAD_DIM), BF16)] * 2,
        ),
        out_shape=[jax.ShapeDtypeStruct((b, n_sub * N_KV, HEAD_DIM), F32)] * 2,
        compiler_params=_cparams(("parallel", "arbitrary")),
        name="compress",
    )(page_table, *([cache_k] * pps), *([cache_v] * pps), _w01(w1_k), _w01(w1_v), _pe2(pe_k), _pe2(pe_v),
      w2_k.astype(BF16), w2_v.astype(BF16))


def _ovl_shifted(n_sub, n_lanes):
    i = (np.arange(n_sub)[:, None] - 1) * CMP_STRIDE
    j = np.arange(n_lanes)[None, :] * SEL_BLOCK
    ov = np.clip(np.minimum(i + CMP_LEN, j + SEL_BLOCK) - np.maximum(i, j), 0, None).astype(np.float32) / CMP_LEN
    ov[0, :] = 0.0
    return jnp.asarray(ov, dtype=BF16)


def _softmax_rows(s, valid):
    s = jnp.where(valid, s, NEG_INF)
    mx = jnp.max(s, axis=-1, keepdims=True)
    mx = jnp.where(mx == NEG_INF, 0.0, mx)
    p = jnp.exp(s - mx)
    den = jnp.maximum(jnp.sum(p, axis=-1, keepdims=True), 1e-30)
    return p * (1.0 / den)


def _select_blocks(imp, qpos, n_sel_blocks):
    r, l = imp.shape
    lane = lax.broadcasted_iota(jnp.int32, (r, l), 1)
    lanef = lane.astype(F32)
    cur = lax.shift_right_logical(qpos, int(np.log2(SEL_BLOCK)))
    valid = lane * SEL_BLOCK <= qpos
    forced = (lane == 0) | ((lane <= cur) & (lane > cur - N_LOCAL))
    score = jnp.where(valid, imp + jnp.where(forced, FORCE_BONUS, 0.0), -FORCE_BONUS)
    score = jnp.where(lane < n_sel_blocks, score, -3e38)
    member = jnp.zeros((r, l), F32)
    picks = []
    for _ in range(N_SEL):
        mx = jnp.max(score, axis=-1, keepdims=True)
        idx = jnp.min(jnp.where(score == mx, lanef, float(l)), axis=-1, keepdims=True)
        hit = lanef == idx
        member = jnp.where(hit, 1.0, member)
        score = jnp.where(hit, NEG_INF, score)
        picks.append(idx)
    return picks, member


def _attn1_kernel(sl_ref, q_ref, kc_ref, vc_ref, ovl_ref, tile_ref, oc_ref, mem_ref, act_ref, *, tq, n_sub,
                  n_sel_blocks):
    qi = pl.program_id(0)
    qpos = qi * tq + lax.broadcasted_iota(jnp.int32, (tq, 1), 0)
    m = lax.broadcasted_iota(jnp.int32, (1, n_sub), 1)
    c_end = (m - 1) * CMP_STRIDE + (CMP_LEN - 1)
    c_ctr = ((m - 1) * CMP_STRIDE).astype(F32) + (CMP_LEN - 1) / 2
    valid = (m >= 1) & (c_end <= qpos)
    dist = jnp.abs(qpos.astype(F32) - c_ctr)
    any_blk = []
    for g in range(N_KV):
        qg = jnp.concatenate([q_ref[:, (g * GROUP + r) * HEAD_DIM:(g * GROUP + r + 1) * HEAD_DIM]
                              for r in range(GROUP)], axis=0).astype(BF16)
        kg = kc_ref[pl.ds(g, n_sub, stride=N_KV), :].astype(BF16)
        s = _dot_t(qg, kg)
        ps = []
        for r in range(GROUP):
            sr = s[r * tq:(r + 1) * tq] * SCALE - sl_ref[g * GROUP + r] * dist
            ps.append(_softmax_rows(sr, valid))
        psum = ps[0]
        for r in range(1, GROUP):
            psum = psum + ps[r]
        pcat = jnp.concatenate(ps, axis=0).astype(BF16)
        oc = jnp.dot(pcat, vc_ref[pl.ds(g, n_sub, stride=N_KV), :].astype(BF16), preferred_element_type=F32)
        for r in range(GROUP):
            oc_ref[:, (g * GROUP + r) * HEAD_DIM:(g * GROUP + r + 1) * HEAD_DIM] = oc[r * tq:(r + 1) * tq]
        imp = jnp.dot(psum.astype(BF16), ovl_ref[...], preferred_element_type=F32)
        _, member = _select_blocks(imp, qpos, n_sel_blocks)
        mem_ref[g] = member.astype(BF16)
        any_blk.append(jnp.max(member, axis=0, keepdims=True))
    hits = jnp.dot(_stack_rows(any_blk, SUBLANE).astype(BF16), tile_ref[...], preferred_element_type=F32)
    act_ref[...] = (hits > 0.5).astype(jnp.int32)


def _prompt_attn1(slopes, proj, kcs, vcs):
    t = proj.shape[0]
    n_sub = kcs.shape[0] // N_KV
    tq = min(ATTN1_TQ, t)
    n_sel_blocks = -(-t // SEL_BLOCK)
    blocks_per_tile = min(SEL_TK, t) // SEL_BLOCK
    assert n_sel_blocks <= LANE and n_sel_blocks <= blocks_per_tile * N_KT_LANES
    ovl = _ovl_shifted(n_sub, LANE)
    blk = np.arange(LANE)[:, None]
    tile_of_block = jnp.asarray((blk // blocks_per_tile == np.arange(LANE)[None, :]) & (blk != 0), dtype=BF16)
    o_c, member, act = pl.pallas_call(
        functools.partial(_attn1_kernel, tq=tq, n_sub=n_sub, n_sel_blocks=n_sel_blocks),
        grid_spec=pltpu.PrefetchScalarGridSpec(
            num_scalar_prefetch=1,
            grid=(t // tq,),
            in_specs=[
                pl.BlockSpec((tq, ATT_WIDTH), lambda i, sl: (i, C_Q // ATT_WIDTH)),
                pl.BlockSpec((n_sub * N_KV, HEAD_DIM), lambda i, sl: (0, 0)),
                pl.BlockSpec((n_sub * N_KV, HEAD_DIM), lambda i, sl: (0, 0)),
                pl.BlockSpec((n_sub, LANE), lambda i, sl: (0, 0)),
                pl.BlockSpec((LANE, LANE), lambda i, sl: (0, 0)),
            ],
            out_specs=[
                pl.BlockSpec((tq, ATT_WIDTH), lambda i, sl: (i, 0)),
                pl.BlockSpec((N_KV, tq, LANE), lambda i, sl: (0, i, 0)),
                pl.BlockSpec((SUBLANE, LANE), lambda i, sl: (i, 0)),
            ],
        ),
        out_shape=[jax.ShapeDtypeStruct((t, ATT_WIDTH), F32), jax.ShapeDtypeStruct((N_KV, t, LANE), BF16),
                   jax.ShapeDtypeStruct((t // tq * SUBLANE, LANE), jnp.int32)],
        compiler_params=_cparams(("parallel",)),
        name="prompt_attn_cmp",
    )(slopes, proj, kcs, vcs, ovl, tile_of_block)
    return o_c, member, act[:, :N_KT_LANES].reshape(-1)


LOG2E = float(np.log2(np.e))


SEL_TQ = 256
SEL_TK = 1024
ATTN1_TQ = 256
N_KT_LANES = LANE // (SEL_TK // SEL_BLOCK)


def _stack_heads(q_ref, tq):
    return jnp.concatenate([(q_ref[:, r * HEAD_DIM:(r + 1) * HEAD_DIM] * (SCALE * LOG2E)).astype(BF16)
                            for r in range(GROUP)], axis=0)


def _sel_kernel(sl_ref, qt_ref, kt_ref, fl_ref, act_ref, q_ref, k_ref, v_ref, mem_ref, et_ref, o_ref,
                qs_ref, m_ref, l_ref, acc_ref, *, tq, tk):
    g = pl.program_id(0)
    step = pl.program_id(1)
    qi = qt_ref[step]
    kt = kt_ref[step]
    flags = fl_ref[step]

    @pl.when((flags & 1) != 0)
    def _():
        qs_ref[...] = _stack_heads(q_ref, tq)
        m_ref[...] = jnp.full(m_ref.shape, NEG_INF, F32)
        l_ref[...] = jnp.zeros(l_ref.shape, F32)
        acc_ref[...] = jnp.zeros(acc_ref.shape, F32)

    sub = tq // ATTN1_TQ
    active = act_ref[((qi * sub) * SUBLANE + g) * N_KT_LANES + kt]
    for h in range(1, sub):
        active = active | act_ref[((qi * sub + h) * SUBLANE + g) * N_KT_LANES + kt]

    def attend(n):
        st = _dot_t(k_ref[0:n, :].astype(BF16), qs_ref[...])
        kpos = kt * tk + lax.broadcasted_iota(jnp.int32, (n, 1), 0)
        qpos = qi * tq + lax.broadcasted_iota(jnp.int32, (1, tq), 1)
        mask = (_dot_t(et_ref[0:n, :], mem_ref[...]) > 0.5) & (kpos <= qpos)
        koff = (kpos - qi * tq).astype(F32)
        vt = v_ref[0:n, :].T.astype(BF16)
        for r in range(GROUP):
            cols = slice(r * tq, (r + 1) * tq)
            sr = jnp.where(mask, st[:, cols] + (sl_ref[g * GROUP + r] * LOG2E) * koff, NEG_INF)
            m_prev = m_ref[:, cols]
            m_new = jnp.maximum(m_prev, jnp.max(sr, axis=0, keepdims=True))
            m_safe = jnp.where(m_new == NEG_INF, 0.0, m_new)
            p = jnp.exp2(sr - m_safe)
            alpha = jnp.exp2(m_prev - m_safe)
            l_ref[:, cols] = alpha * l_ref[:, cols] + jnp.sum(p, axis=0, keepdims=True)
            acc_ref[:, cols] = alpha * acc_ref[:, cols] + jnp.dot(vt, p.astype(BF16), preferred_element_type=F32)
            m_ref[:, cols] = m_new

    @pl.when(active != 0)
    def _():
        attend(tk)

    @pl.when((active == 0) & (kt == 0))
    def _():
        attend(LANE)

    @pl.when((flags & 2) != 0)
    def _():
        for r in range(GROUP):
            cols = slice(r * tq, (r + 1) * tq)
            o = acc_ref[:, cols] * (1.0 / jnp.maximum(l_ref[:, cols], 1e-30))
            o_ref[:, r * HEAD_DIM:(r + 1) * HEAD_DIM] = o.T


def _sel_steps(t, tq, tk):
    qt, kt, fl = [], [], []
    for qi in range(t // tq):
        hi = (qi * tq + tq - 1) // tk
        for k in range(hi + 1):
            qt.append(qi)
            kt.append(k)
            fl.append((1 if k == 0 else 0) | (2 if k == hi else 0))
    as_i32 = lambda a: jnp.asarray(np.asarray(a, np.int32))
    return as_i32(qt), as_i32(kt), as_i32(fl)


def _prompt_sel(slopes, proj, member, active, emat_t):
    t = proj.shape[0]
    tq, tk = min(SEL_TQ, t), min(SEL_TK, t)
    qt, kt, fl = _sel_steps(t, tq, tk)
    imap = lambda f: (lambda g, s, sl, qt, kt, fl, act: f(g, qt[s], kt[s]))
    return pl.pallas_call(
        functools.partial(_sel_kernel, tq=tq, tk=tk),
        grid_spec=pltpu.PrefetchScalarGridSpec(
            num_scalar_prefetch=5,
            grid=(N_KV, int(qt.shape[0])),
            in_specs=[
                pl.BlockSpec((tq, GROUP * HEAD_DIM), imap(lambda g, qi, ki: (qi, C_Q // (GROUP * HEAD_DIM) + g))),
                pl.BlockSpec((tk, HEAD_DIM), imap(lambda g, qi, ki: (ki, C_KS // HEAD_DIM + g))),
                pl.BlockSpec((tk, HEAD_DIM), imap(lambda g, qi, ki: (ki, C_VS // HEAD_DIM + g))),
                pl.BlockSpec((None, tq, LANE), imap(lambda g, qi, ki: (g, qi, 0))),
                pl.BlockSpec((tk, LANE), imap(lambda g, qi, ki: (ki, 0))),
            ],
            out_specs=pl.BlockSpec((tq, GROUP * HEAD_DIM), imap(lambda g, qi, ki: (qi, g))),
            scratch_shapes=[
                pltpu.VMEM((GROUP * tq, HEAD_DIM), BF16),
                pltpu.VMEM((1, GROUP * tq), F32),
                pltpu.VMEM((1, GROUP * tq), F32),
                pltpu.VMEM((HEAD_DIM, GROUP * tq), F32),
            ],
        ),
        out_shape=jax.ShapeDtypeStruct((t, ATT_WIDTH), F32),
        compiler_params=_cparams(("parallel", "arbitrary")),
        name="prompt_attn_sel",
    )(slopes, qt, kt, fl, active, proj, proj, proj, member, emat_t)


def _win_kernel(sl_ref, q_ref, *refs, tq, n_kt):
    k_refs, v_refs, o_ref = refs[:n_kt], refs[n_kt:2 * n_kt], refs[2 * n_kt]
    g = pl.program_id(0)
    qi = pl.program_id(1)
    qs = _stack_heads(q_ref, tq)
    k = jnp.concatenate([kr[...] for kr in k_refs], axis=0).astype(BF16)
    vt = jnp.concatenate([vr[...] for vr in v_refs], axis=0).T.astype(BF16)
    st = _dot_t(k, qs)
    kpos = (qi - (n_kt - 1)) * tq + lax.broadcasted_iota(jnp.int32, (n_kt * tq, 1), 0)
    qpos = qi * tq + lax.broadcasted_iota(jnp.int32, (1, tq), 1)
    rel = kpos - qpos
    mask = (rel <= 0) & (rel > -WINDOW) & (kpos >= 0)
    koff = (kpos - qi * tq).astype(F32)
    for r in range(GROUP):
        sr = jnp.where(mask, st[:, r * tq:(r + 1) * tq] + (sl_ref[g * GROUP + r] * LOG2E) * koff, NEG_INF)
        mx = jnp.max(sr, axis=0, keepdims=True)
        p = jnp.exp2(sr - jnp.where(mx == NEG_INF, 0.0, mx))
        den = jnp.maximum(jnp.sum(p, axis=0, keepdims=True), 1e-30)
        o = jnp.dot(vt, p.astype(BF16), preferred_element_type=F32) * (1.0 / den)
        o_ref[:, r * HEAD_DIM:(r + 1) * HEAD_DIM] = o.T


def _prompt_win(slopes, proj):
    t = proj.shape[0]
    tq = min(SEL_TQ, t)
    n_kt = -(-(WINDOW - 1) // tq) + 1
    kv_spec = lambda col, j: pl.BlockSpec(
        (tq, HEAD_DIM), lambda g, qi, sl: (jnp.maximum(qi - (n_kt - 1) + j, 0), col // HEAD_DIM + g))
    return pl.pallas_call(
        functools.partial(_win_kernel, tq=tq, n_kt=n_kt),
        grid_spec=pltpu.PrefetchScalarGridSpec(
            num_scalar_prefetch=1,
            grid=(N_KV, t // tq),
            in_specs=[pl.BlockSpec((tq, GROUP * HEAD_DIM), lambda g, qi, sl: (qi, C_Q // (GROUP * HEAD_DIM) + g))]
            + [kv_spec(C_KW, j) for j in range(n_kt)] + [kv_spec(C_VW, j) for j in range(n_kt)],
            out_specs=pl.BlockSpec((tq, GROUP * HEAD_DIM), lambda g, qi, sl: (qi, g)),
        ),
        out_shape=jax.ShapeDtypeStruct((t, ATT_WIDTH), F32),
        compiler_params=_cparams(("parallel", "parallel")),
        name="prompt_attn_win",
    )(slopes, *([proj] * (1 + 2 * n_kt)))


def _sel_expand_matrix_t(t):
    e = (np.arange(t)[:, None] // SEL_BLOCK) == np.arange(LANE)[None, :]
    return jnp.asarray(e, dtype=BF16)


def _sattn1_kernel(sl_ref, q_ref, kc_ref, vc_ref, ovl_ref, oc_ref, idx_ref, *, n_sub, qpos, n_sel_blocks, n_lanes):
    m = lax.broadcasted_iota(jnp.int32, (1, n_sub), 1)
    c_end = (m - 1) * CMP_STRIDE + (CMP_LEN - 1)
    c_ctr = ((m - 1) * CMP_STRIDE).astype(F32) + (CMP_LEN - 1) / 2
    valid = (m >= 1) & (c_end <= qpos)
    dist = jnp.abs(float(qpos) - c_ctr)
    qposv = jnp.full((SUBLANE, 1), qpos, jnp.int32)
    out_lane = lax.broadcasted_iota(jnp.int32, (SUBLANE, LANE), 1)
    idx_acc = jnp.zeros((SUBLANE, LANE), F32)
    head_row = lax.broadcasted_iota(jnp.int32, (SUBLANE, 1), 0)
    psums = []
    for g in range(N_KV):
        qg = _stack_rows([q_ref[:, (g * GROUP + r) * HEAD_DIM:(g * GROUP + r + 1) * HEAD_DIM]
                          for r in range(GROUP)], SUBLANE).astype(BF16)
        s = _dot_t(qg, kc_ref[pl.ds(g, n_sub, stride=N_KV), :].astype(BF16))
        slope = jnp.zeros((SUBLANE, 1), F32)
        for r in range(GROUP):
            slope = jnp.where(head_row == r, sl_ref[g * GROUP + r], slope)
        p = _softmax_rows(s * SCALE - slope * dist, valid)
        p = jnp.where(head_row < GROUP, p, 0.0)
        oc = jnp.dot(p.astype(BF16), vc_ref[pl.ds(g, n_sub, stride=N_KV), :].astype(BF16),
                     preferred_element_type=F32)
        for r in range(GROUP):
            oc_ref[:, (g * GROUP + r) * HEAD_DIM:(g * GROUP + r + 1) * HEAD_DIM] = _take_row(oc, r)
        psums.append(jnp.sum(p, axis=0, keepdims=True))
    imp = jnp.dot(_stack_rows(psums, SUBLANE).astype(BF16), ovl_ref[...], preferred_element_type=F32)
    picks, _ = _select_blocks(imp, qposv, n_sel_blocks)
    for j, pk in enumerate(picks):
        idx_acc = jnp.where(out_lane == j, pk, idx_acc)
    idx_ref[...] = idx_acc.astype(jnp.int32)


def _sample_attn1(slopes, sproj3, kcs, vcs, past_len):
    b = sproj3.shape[0]
    n_sub = kcs.shape[1] // N_KV
    n_sel_blocks = -(-(past_len + 1) // SEL_BLOCK)
    n_lanes = -(-n_sel_blocks // LANE) * LANE
    ovl = _ovl_shifted(n_sub, n_lanes)
    return pl.pallas_call(
        functools.partial(_sattn1_kernel, n_sub=n_sub, qpos=past_len, n_sel_blocks=n_sel_blocks, n_lanes=n_lanes),
        grid_spec=pltpu.PrefetchScalarGridSpec(
            num_scalar_prefetch=1,
            grid=(b,),
            in_specs=[
                pl.BlockSpec((None, 1, ATT_WIDTH), lambda i, sl: (i, 0, C_Q // ATT_WIDTH)),
                pl.BlockSpec((None, n_sub * N_KV, HEAD_DIM), lambda i, sl: (i, 0, 0)),
                pl.BlockSpec((None, n_sub * N_KV, HEAD_DIM), lambda i, sl: (i, 0, 0)),
                pl.BlockSpec((n_sub, n_lanes), lambda i, sl: (0, 0)),
            ],
            out_specs=[
                pl.BlockSpec((None, 1, ATT_WIDTH), lambda i, sl: (i, 0, 0)),
                pl.BlockSpec((None, SUBLANE, LANE), lambda i, sl: (i, 0, 0)),
            ],
        ),
        out_shape=[jax.ShapeDtypeStruct((b, 1, ATT_WIDTH), F32), jax.ShapeDtypeStruct((b, SUBLANE, LANE), jnp.int32)],
        compiler_params=_cparams(("parallel",)),
        name="sample_attn_cmp",
    )(slopes, sproj3, kcs, vcs, ovl)


SEL_PER_STEP = 16
BLK_ROWS = SEL_BLOCK * N_KV


def _sattn2_kernel(idx_ref, pt_ref, sl_ref, q_ref, ksn_ref, vsn_ref, kwn_ref, vwn_ref, *refs, past_len, w_buf):
    del pt_ref
    nblk = N_KV * SEL_PER_STEP
    kbs, vbs = refs[:nblk], refs[nblk:2 * nblk]
    kwc_ref, vwc_ref, os_ref, ow_ref, kwo_ref, vwo_ref, qs_ref, m_ref, l_ref, acc_ref = refs[2 * nblk:]
    b = pl.program_id(0)
    jt = pl.program_id(1)
    head_row = lax.broadcasted_iota(jnp.int32, (SUBLANE, 1), 0)

    def slope_col(g):
        slope = jnp.zeros((SUBLANE, 1), F32)
        for r in range(GROUP):
            slope = jnp.where(head_row == r, sl_ref[g * GROUP + r], slope)
        return slope

    @pl.when(jt == 0)
    def _():
        for g in range(N_KV):
            qs_ref[g * SUBLANE:(g + 1) * SUBLANE, :] = _stack_rows(
                [q_ref[:, (g * GROUP + r) * HEAD_DIM:(g * GROUP + r + 1) * HEAD_DIM] for r in range(GROUP)],
                SUBLANE)
        m_ref[...] = jnp.full(m_ref.shape, NEG_INF, F32)
        l_ref[...] = jnp.zeros(l_ref.shape, F32)
        acc_ref[...] = jnp.zeros(acc_ref.shape, F32)

    n_keys = SEL_PER_STEP * SEL_BLOCK
    lane = lax.broadcasted_iota(jnp.int32, (1, n_keys), 1)
    pos_in_blk = lax.broadcasted_iota(jnp.int32, (SEL_BLOCK, 1), 0)
    for g in range(N_KV):
        rows = slice(g * SUBLANE, (g + 1) * SUBLANE)
        kparts, vparts = [], []
        blkv = jnp.zeros((1, n_keys), jnp.int32)
        for jj in range(SEL_PER_STEP):
            blk = idx_ref[b, g, jt * SEL_PER_STEP + jj]
            is_past = (blk * SEL_BLOCK + pos_in_blk) < past_len
            new_k = jnp.broadcast_to(ksn_ref[:, g * HEAD_DIM:(g + 1) * HEAD_DIM], (SEL_BLOCK, HEAD_DIM))
            new_v = jnp.broadcast_to(vsn_ref[:, g * HEAD_DIM:(g + 1) * HEAD_DIM], (SEL_BLOCK, HEAD_DIM))
            kparts.append(jnp.where(is_past, kbs[g * SEL_PER_STEP + jj][pl.ds(g, SEL_BLOCK, stride=N_KV), :], new_k))
            vparts.append(jnp.where(is_past, vbs[g * SEL_PER_STEP + jj][pl.ds(g, SEL_BLOCK, stride=N_KV), :], new_v))
            blkv = jnp.where(lax.shift_right_logical(lane, int(np.log2(SEL_BLOCK))) == jj, blk, blkv)
        kcat = jnp.concatenate(kparts, axis=0).astype(BF16)
        vcat = jnp.concatenate(vparts, axis=0).astype(BF16)
        rel = blkv * SEL_BLOCK + (lane & (SEL_BLOCK - 1)) - past_len
        s = _dot_t(qs_ref[rows, :].astype(BF16), kcat) * SCALE + slope_col(g) * rel.astype(F32)
        s = jnp.where(rel <= 0, s, NEG_INF)
        m_prev = m_ref[rows]
        m_new = jnp.maximum(m_prev, jnp.max(s, axis=-1, keepdims=True))
        m_safe = jnp.where(m_new == NEG_INF, 0.0, m_new)
        p = jnp.exp(s - m_safe)
        alpha = jnp.exp(m_prev - m_safe)
        l_ref[rows] = alpha * l_ref[rows] + jnp.sum(p, axis=-1, keepdims=True)
        acc_ref[rows] = alpha * acc_ref[rows] + jnp.dot(p.astype(BF16), vcat, preferred_element_type=F32)
        m_ref[rows] = m_new

    @pl.when(jt == N_SEL // SEL_PER_STEP - 1)
    def _():
        i = lax.broadcasted_iota(jnp.int32, (1, w_buf + SUBLANE), 1)
        relw = jnp.minimum(i, w_buf) - w_buf
        validw = (relw > -WINDOW) & (i <= w_buf)
        for g in range(N_KV):
            rows = slice(g * SUBLANE, (g + 1) * SUBLANE)
            o = acc_ref[rows] * (1.0 / jnp.maximum(l_ref[rows], 1e-30))
            new_k = jnp.broadcast_to(kwn_ref[:, g * HEAD_DIM:(g + 1) * HEAD_DIM], (SUBLANE, HEAD_DIM))
            new_v = jnp.broadcast_to(vwn_ref[:, g * HEAD_DIM:(g + 1) * HEAD_DIM], (SUBLANE, HEAD_DIM))
            kw = jnp.concatenate([kwc_ref[pl.ds(g, w_buf, stride=N_KV), :], new_k], axis=0).astype(BF16)
            vw = jnp.concatenate([vwc_ref[pl.ds(g, w_buf, stride=N_KV), :], new_v], axis=0).astype(BF16)
            sw = _dot_t(qs_ref[rows, :].astype(BF16), kw) * SCALE + slope_col(g) * relw.astype(F32)
            ow = jnp.dot(_softmax_rows(sw, validw).astype(BF16), vw, preferred_element_type=F32)
            for r in range(GROUP):
                cols = slice((g * GROUP + r) * HEAD_DIM, (g * GROUP + r + 1) * HEAD_DIM)
                os_ref[:, cols] = _take_row(o, r)
                ow_ref[:, cols] = _take_row(ow, r)
        n_rows = w_buf * N_KV
        rowi = lax.broadcasted_iota(jnp.int32, (n_rows, HEAD_DIM), 0)
        for src, new, dst in ((kwc_ref, kwn_ref, kwo_ref), (vwc_ref, vwn_ref, vwo_ref)):
            out = pltpu.roll(src[...], n_rows - N_KV, 0)
            for g in range(N_KV):
                out = jnp.where(rowi == n_rows - N_KV + g, new[:, g * HEAD_DIM:(g + 1) * HEAD_DIM], out)
            dst[...] = out


def _sample_attn2(idx, page_table, slopes, sproj3, pool_k, pool_v, win_k, win_v, past_len):
    b = sproj3.shape[0]
    w_buf = win_k.shape[0] // (b * N_KV)
    n_blk = past_len // SEL_BLOCK
    bpp = PAGE_SIZE // SEL_BLOCK

    def blk_spec(g, jj):
        def index_map(bb, jt, ix, pt, sl):
            blk = jnp.minimum(ix[bb, g, jt * SEL_PER_STEP + jj], n_blk - 1)
            return (pt[bb, blk // bpp] * bpp + blk % bpp, 0)
        return pl.BlockSpec((BLK_ROWS, HEAD_DIM), index_map)

    blk_specs = [blk_spec(g, jj) for g in range(N_KV) for jj in range(SEL_PER_STEP)]
    row = lambda col, w: pl.BlockSpec((None, 1, w), lambda bb, jt, ix, pt, sl: (bb, 0, col // w))
    win_spec = pl.BlockSpec((w_buf * N_KV, HEAD_DIM), lambda bb, jt, ix, pt, sl: (bb, 0))
    o_spec = pl.BlockSpec((None, 1, ATT_WIDTH), lambda bb, jt, ix, pt, sl: (bb, 0, 0))
    n_q = N_KV * SUBLANE
    return pl.pallas_call(
        functools.partial(_sattn2_kernel, past_len=past_len, w_buf=w_buf),
        grid_spec=pltpu.PrefetchScalarGridSpec(
            num_scalar_prefetch=3,
            grid=(b, N_SEL // SEL_PER_STEP),
            in_specs=[row(C_Q, ATT_WIDTH), row(C_KS, KV_WIDTH), row(C_VS, KV_WIDTH), row(C_KW, KV_WIDTH),
                      row(C_VW, KV_WIDTH)] + blk_specs + blk_specs + [win_spec, win_spec],
            out_specs=[o_spec, o_spec, win_spec, win_spec],
            scratch_shapes=[
                pltpu.VMEM((n_q, HEAD_DIM), F32),
                pltpu.VMEM((n_q, 1), F32),
                pltpu.VMEM((n_q, 1), F32),
                pltpu.VMEM((n_q, HEAD_DIM), F32),
            ],
        ),
        out_shape=[jax.ShapeDtypeStruct((b, 1, ATT_WIDTH), F32)] * 2
        + [jax.ShapeDtypeStruct(win_k.shape, F32)] * 2,
        compiler_params=_cparams(("parallel", "arbitrary")),
        name="sample_attn_sel_win",
    )(idx, page_table, slopes, *([sproj3] * 5), *([pool_k] * (N_KV * SEL_PER_STEP)),
      *([pool_v] * (N_KV * SEL_PER_STEP)), win_k, win_v)


def _merge_kernel(oc_ref, os_ref, ow_ref, gl_ref, za_ref, mgc_ref, mga_ref, cd_ref, x_ref, wa_ref, wo_ref, gp_ref,
                  y_ref):
    gates = _sigmoid(gl_ref[...])
    tm = gates.shape[0]
    pieces = []
    for h in range(N_HEADS):
        cols = slice(h * HEAD_DIM, (h + 1) * HEAD_DIM)
        gate = lambda c: jnp.broadcast_to(gates[:, c * N_HEADS + h:c * N_HEADS + h + 1], (tm, HEAD_DIM))
        pieces.append(gate(0) * oc_ref[:, cols] + gate(1) * os_ref[:, cols] + gate(2) * ow_ref[:, cols])
    o = jnp.concatenate(pieces, axis=1)
    attn_d = jnp.dot((o * _silu(za_ref[...])).astype(BF16), wa_ref[...], preferred_element_type=F32)
    mix = _sigmoid(mgc_ref[...]) * cd_ref[...] + _sigmoid(mga_ref[...]) * attn_d
    z = jnp.dot(mix.astype(BF16), wo_ref[...], preferred_element_type=F32)
    ms = jnp.mean(z * z, axis=-1, keepdims=True)
    y_ref[...] = x_ref[...] + z * lax.rsqrt(ms + RMS_EPS) * gp_ref[...]


def _merge(o_c, o_s, o_w, proj, conv_d, x2d, w_ao, w_o, g_post):
    m = x2d.shape[0]
    tm = min(128, m)
    wide = lambda col: pl.BlockSpec((tm, D_MODEL), lambda i: (i, col // D_MODEL))
    plain = pl.BlockSpec((tm, D_MODEL), lambda i: (i, 0))
    return pl.pallas_call(
        _merge_kernel,
        grid=(m // tm,),
        in_specs=[plain, plain, plain,
                  pl.BlockSpec((tm, LANE), lambda i: (i, C_GL // LANE)),
                  wide(C_ZA), wide(C_MGC), wide(C_MGA), plain, plain,
                  _const_spec((ATT_WIDTH, D_MODEL)), _const_spec((D_MODEL, D_MODEL)), _const_spec((1, D_MODEL))],
        out_specs=plain,
        out_shape=jax.ShapeDtypeStruct((m, D_MODEL), F32),
        compiler_params=_cparams(("parallel",)),
        name="merge",
    )(o_c, o_s, o_w, proj, proj, proj, proj, conv_d, x2d, w_ao, w_o, g_post.reshape(1, D_MODEL))


def _alibi_slopes():
    h = jnp.arange(1, N_HEADS + 1, dtype=F32)
    return jnp.exp2(-8.0 * h / N_HEADS)


def _layer_weights(l, w_in, w_dw, w_conv_out, w_attn_out, w_out):
    w_dw_p = jnp.concatenate([w_dw[l], jnp.zeros((HALO - CONV_K, CONV_C), F32)], axis=0)
    return (_proj_weights(w_in[l]), w_dw_p, w_conv_out[l].astype(BF16), w_attn_out[l].astype(BF16),
            w_out[l].astype(BF16))


def _prompt_layer(x, slopes, w_r, w_dw_p, w_co, w_ao, w_o, g_pre, b_dw, ln_g, ln_b, pe_k, w1_k, w2_k, pe_v, w1_v,
                  w2_v, g_post):
    t = x.shape[1]
    x2d = x.reshape(t, D_MODEL)
    proj = _project(x2d, g_pre, *w_r)
    conv_d, tail = _conv_branch(proj, w_dw_p, b_dw, ln_g, ln_b, w_co)
    kv = _kv_rows(proj)
    n_pages = t // PAGE_SIZE
    pt = jnp.arange(n_pages, dtype=jnp.int32).reshape(1, n_pages)
    kcs, vcs = _compress(pt, kv[0], kv[1], pe_k, w1_k, w2_k, pe_v, w1_v, w2_v, pps=min(16, n_pages))
    o_c, member, active = _prompt_attn1(slopes, proj, kcs[0], vcs[0])
    o_s = _prompt_sel(slopes, proj, member, active, _sel_expand_matrix_t(t))
    o_w = _prompt_win(slopes, proj)
    y = _merge(o_c, o_s, o_w, proj, conv_d, x2d, w_ao, w_o, g_post)
    wl = min(WINDOW, t)
    heads = lambda a: a.reshape(1, -1, N_KV, HEAD_DIM)
    states = (heads(kv[0]), heads(kv[1]), heads(kv[2]), heads(kv[3]), heads(kv[4][-wl * N_KV:]),
              heads(kv[5][-wl * N_KV:]), tail[-(CONV_K - 1):].reshape(1, CONV_K - 1, CONV_C))
    return y.reshape(1, t, D_MODEL), states


def _sample_layer(x, l, ck_cmp, cv_cmp, ck_slc, cv_slc, ck_win, cv_win, c_conv, page_table, slopes, w_r, w_dw_p,
                  w_co, w_ao, w_o, g_pre, b_dw, ln_g, ln_b, pe_k, w1_k, w2_k, pe_v, w1_v, w2_v, g_post):
    b = x.shape[0]
    n_pool = ck_cmp.shape[1]
    n_pages = page_table.shape[1]
    past_len = n_pages * PAGE_SIZE
    w_buf = ck_win.shape[2]
    x2d = x.reshape(b, D_MODEL)
    sproj = _project(x2d, g_pre, *w_r)
    sproj3 = sproj.reshape(b, 1, PROJ_W)
    conv_d, s_conv_t = _sample_conv(jnp.transpose(c_conv[l], (1, 0, 2)), sproj, w_dw_p, b_dw, ln_g, ln_b, w_co)
    s_conv = jnp.transpose(s_conv_t, (1, 0, 2))
    pool = lambda a: a[l].reshape(n_pool * PAGE_ROWS, HEAD_DIM)
    win = lambda a: a[l].reshape(b * w_buf * N_KV, HEAD_DIM)
    kcs, vcs = _compress(page_table, pool(ck_cmp), pool(cv_cmp), pe_k, w1_k, w2_k, pe_v, w1_v, w2_v,
                         pps=min(32, n_pages))
    o_c, idx = _sample_attn1(slopes, sproj3, kcs, vcs, past_len)
    o_s, o_w, kwo, vwo = _sample_attn2(idx[:, :N_KV, :N_SEL], page_table, slopes, sproj3, pool(ck_slc), pool(cv_slc),
                                       win(ck_win), win(cv_win), past_len)
    y = _merge(o_c.reshape(b, ATT_WIDTH), o_s.reshape(b, ATT_WIDTH), o_w.reshape(b, ATT_WIDTH), sproj, conv_d, x2d,
               w_ao, w_o, g_post)
    heads = lambda a: a.reshape(b, -1, N_KV, HEAD_DIM)
    new = _kv_rows(sproj)
    states = (heads(new[0]), heads(new[1]), heads(new[2]), heads(new[3]), heads(kwo), heads(vwo), s_conv)
    return y.reshape(b, 1, D_MODEL), states


def kernel(x_prompt, x_sample, cache_k_cmp, cache_v_cmp, cache_k_slc, cache_v_slc, cache_k_win, cache_v_win,
           cache_conv, page_table, g_pre, w_in, w_dw, b_dw, ln_g, ln_b, w_conv_out, pe_k, w1_k, w2_k, pe_v, w1_v,
           w2_v, w_attn_out, w_out, g_post):
    depth = g_pre.shape[0]
    slopes = _alibi_slopes()
    y_prompt, y_sample = x_prompt, x_sample
    p_states, s_states = [], []
    for l in range(depth):
        mats = _layer_weights(l, w_in, w_dw, w_conv_out, w_attn_out, w_out)
        vecs = (g_pre[l], b_dw[l], ln_g[l], ln_b[l], pe_k[l], w1_k[l], w2_k[l], pe_v[l], w1_v[l], w2_v[l], g_post[l])
        y_prompt, sp = _prompt_layer(y_prompt, slopes, *mats, *vecs)
        y_sample, ss = _sample_layer(y_sample, l, cache_k_cmp, cache_v_cmp, cache_k_slc, cache_v_slc, cache_k_win,
                                     cache_v_win, cache_conv, page_table, slopes, *mats, *vecs)
        p_states.append(sp)
        s_states.append(ss)
    p_out = [jnp.stack(a) for a in zip(*p_states)]
    s_out = [jnp.stack(a) for a in zip(*s_states)]
    return (y_prompt, y_sample, *p_out, *s_out)
```
